```python
import jax
import jax.numpy as jnp
from jax import lax
import numpy as np

D_MODEL = 2048
BATCH = 16
SEQ = 2048
DEPTH = 4

CTX_LEN = 256
GRID_W = 64
HEAD_DIM = 128
GROUP_HEADS = D_MODEL // (2 * HEAD_DIM)
GROUP_W = GROUP_HEADS * HEAD_DIM
GQA_KV_HEADS = GROUP_HEADS // 4
NA_WIN_R = 8
NA_WIN_C = 16
CHUNK = 64
Q_BLOCK = 128
ROPE_BASE = 10000.0
N_EXPERTS = 16
N_GROUPS = 4
EXPERTS_PER_GROUP = N_EXPERTS // N_GROUPS
TOP_K = 2
D_FF_EXPERT = D_MODEL // 2
N_MOD = 6
EPS = 1e-6
AB_SIZES = (GROUP_W, GROUP_W, GROUP_W, GROUP_W, 4 * GROUP_HEADS, GROUP_W, GROUP_W, GROUP_W)
CD_SIZES = (GROUP_W, GQA_KV_HEADS * HEAD_DIM, GQA_KV_HEADS * HEAD_DIM, GROUP_W, GROUP_W, GROUP_W, GROUP_W)

kernel_name = 'hybrid_mlstm_natten_gqa_retention_moe_dit'


def _split(z, sizes):
    return jnp.split(z, np.cumsum(sizes)[:-1].tolist(), axis=-1)


def _heads(t, n_heads):
    B, S, _ = t.shape
    return t.reshape(B, S, n_heads, -1).transpose(0, 2, 1, 3)


def _merge(t):
    B, H, S, d = t.shape
    return t.transpose(0, 2, 1, 3).reshape(B, S, H * d)


def _flip(t):
    return jnp.flip(t, axis=2)


def _ident(t):
    return t


def rms_norm(x, g):
    xf = x.astype(jnp.float32)
    y = xf * lax.rsqrt(jnp.mean(xf * xf, axis=-1, keepdims=True) + EPS)
    return (y * g.astype(jnp.float32)).astype(x.dtype)


def axial_rope(n_tokens, dim):
    t = jnp.arange(n_tokens)
    row = (t // GRID_W).astype(jnp.float32)
    col = (t % GRID_W).astype(jnp.float32)
    n_axis = dim // 4
    inv = ROPE_BASE ** (-jnp.arange(n_axis, dtype=jnp.float32) / n_axis)
    ang = jnp.concatenate([row[:, None] * inv, col[:, None] * inv], axis=-1)
    return jnp.cos(ang), jnp.sin(ang)


def apply_rope(x, cos, sin):
    x1, x2 = jnp.split(x, 2, axis=-1)
    cos = cos.astype(x.dtype)
    sin = sin.astype(x.dtype)
    return jnp.concatenate([x1 * cos - x2 * sin, x1 * sin + x2 * cos], axis=-1)


def mlstm_scan(q, k, v, log_i, log_f, state):
    B, H, S, d = q.shape
    nc = S // CHUNK

    def chunks(t):
        return jnp.moveaxis(t.reshape((B, H, nc, CHUNK) + t.shape[3:]), 2, 0)

    tri = jnp.tril(jnp.ones((CHUNK, CHUNK), dtype=bool))

    def step(carry, inp):
        C, n, m = carry
        qc, kc, vc, li, lf = inp
        qc, kc, vc = qc.astype(jnp.float32), kc.astype(jnp.float32), vc.astype(jnp.float32)
        b = jnp.cumsum(lf, axis=-1)
        log_w = jnp.where(tri, b[..., :, None] - b[..., None, :] + li[..., None, :], -jnp.inf)
        log_inter = b + m[..., None]
        m_row = jnp.maximum(log_inter, jnp.max(log_w, axis=-1))
        w = jnp.exp(log_w - m_row[..., None])
        w_inter = jnp.exp(log_inter - m_row)
        s = jnp.einsum('bhtd,bhsd->bhts', qc, kc) * w
        num = jnp.einsum('bhts,bhsd->bhtd', s, vc) + w_inter[..., None] * jnp.einsum('bhtk,bhkv->bhtv', qc, C)
        den = jnp.sum(s, axis=-1) + w_inter * jnp.einsum('bhtk,bhk->bht', qc, n)
        h = num / jnp.maximum(jnp.abs(den), jnp.exp(-m_row))[..., None]
        b_end = b[..., -1]
        log_end = b_end[..., None] - b + li
        m_new = jnp.maximum(b_end + m, jnp.max(log_end, axis=-1))
        w_end = jnp.exp(log_end - m_new[..., None])
        w_prev = jnp.exp(b_end + m - m_new)
        C_new = w_prev[..., None, None] * C + jnp.einsum('bhsk,bhsv->bhkv', kc * w_end[..., None], vc)
        n_new = w_prev[..., None] * n + jnp.einsum('bhs,bhsk->bhk', w_end, kc)
        return (C_new, n_new, m_new), h

    state, hs = lax.scan(step, state, (chunks(q), chunks(k), chunks(v), chunks(log_i), chunks(log_f)))
    return jnp.moveaxis(hs, 0, 2).reshape(B, H, S, d).astype(q.dtype), state


def retention_scan(q, k, v, log_gamma, state):
    B, H, S, d = q.shape
    nc = S // CHUNK
    pos = jnp.arange(CHUNK, dtype=jnp.float32)
    lg = log_gamma[:, None]
    rel = pos[:, None] - pos[None, :]
    decay_intra = jnp.exp(jnp.maximum(rel, 0.0)[None] * lg[..., None]) * (rel >= 0)[None]
    decay_q = jnp.exp((pos + 1.0)[None] * lg)
    decay_k = jnp.exp((CHUNK - 1.0 - pos)[None] * lg)
    decay_chunk = jnp.exp(CHUNK * log_gamma)

    def chunks(t):
        return jnp.moveaxis(t.reshape(B, H, nc, CHUNK, d), 2, 0)

    def step(S_prev, inp):
        qc, kc, vc = (t.astype(jnp.float32) for t in inp)
        s = jnp.einsum('bhtd,bhsd->bhts', qc, kc) * decay_intra
        o = jnp.einsum('bhts,bhsd->bhtd', s, vc) + decay_q[..., None] * jnp.einsum('bhtk,bhkv->bhtv', qc, S_prev)
        S_new = decay_chunk[:, None, None] * S_prev + jnp.einsum('bhsk,bhsv->bhkv', kc * decay_k[..., None], vc)
        return S_new, o

    state, os_ = lax.scan(step, state, (chunks(q), chunks(k), chunks(v)))
    return jnp.moveaxis(os_, 0, 2).reshape(B, H, S, d).astype(q.dtype), state


def block_attention(q, k, v):
    B, Hq, S, d = q.shape
    Hk = k.shape[1]
    G = Hq // Hk
    nb = S // Q_BLOCK
    qb = jnp.moveaxis(q.reshape(B, Hk, G, nb, Q_BLOCK, d), 3, 0)
    scale = d ** -0.5

    def one_block(qi):
        s = jnp.einsum('bkgqd,bktd->bkgqt', qi, k).astype(jnp.float32) * scale
        p = jax.nn.softmax(s, axis=-1).astype(v.dtype)
        return jnp.einsum('bkgqt,bktd->bkgqd', p, v)

    o = lax.map(one_block, qb)
    return jnp.moveaxis(o, 0, 3).reshape(B, Hq, S, d)


def neighbourhood_attention(q, k, v, k_ctx, v_ctx, rpb):
    B, H, S, d = q.shape
    rows = S // GRID_W
    wr = min(NA_WIN_R, rows)
    scale = d ** -0.5
    kg = k.reshape(B, H, rows, GRID_W, d)
    vg = v.reshape(B, H, rows, GRID_W, d)
    q_rows = jnp.moveaxis(q.reshape(B, H, rows, GRID_W, d), 2, 0)
    col = jnp.arange(GRID_W)
    c0 = jnp.clip(col - NA_WIN_C // 2, 0, GRID_W - NA_WIN_C)
    col_ok = (col[None, :] >= c0[:, None]) & (col[None, :] < c0[:, None] + NA_WIN_C)
    dc_idx = jnp.clip(col[None, :] - col[:, None], -(NA_WIN_C - 1), NA_WIN_C - 1) + NA_WIN_C - 1
    bias_c = rpb[:, :, dc_idx]

    def one_row(args):
        r, q_r = args
        r0 = jnp.clip(r - wr // 2, 0, rows - wr)
        k_b = lax.dynamic_slice_in_dim(kg, r0, wr, axis=2)
        v_b = lax.dynamic_slice_in_dim(vg, r0, wr, axis=2)
        dr_idx = r0 + jnp.arange(wr) - r + NA_WIN_R - 1
        bias = jnp.take(bias_c, dr_idx, axis=1).transpose(0, 2, 1, 3)
        s_lat = jnp.einsum('bhqd,bhrkd->bhqrk', q_r, k_b).astype(jnp.float32) * scale + bias
        s_lat = jnp.where(col_ok[:, None, :], s_lat, -jnp.inf)
        s_ctx = jnp.einsum('bhqd,bhcd->bhqc', q_r, k_ctx).astype(jnp.float32) * scale
        n_lat = wr * GRID_W
        p = jax.nn.softmax(jnp.concatenate([s_lat.reshape(B, H, GRID_W, n_lat), s_ctx], axis=-1), axis=-1)
        p = p.astype(v.dtype)
        p_lat = p[..., :n_lat].reshape(B, H, GRID_W, wr, GRID_W)
        return (jnp.einsum('bhqrk,bhrkd->bhqd', p_lat, v_b)
                + jnp.einsum('bhqc,bhcd->bhqd', p[..., n_lat:], v_ctx))

    out = lax.map(one_row, (jnp.arange(rows), q_rows))
    return jnp.moveaxis(out, 0, 2).reshape(B, H, S, d)


def mixer_ab(h_lat, h_ctx, w_in, b_gate, g_head, rpb, w_out):
    H, d = GROUP_HEADS, HEAD_DIM
    B = h_lat.shape[0]

    def project(h):
        mq, mk, mv, mo, gates, nq, nk, nv = _split(h @ w_in, AB_SIZES)
        gates = gates.astype(jnp.float32) + b_gate
        m = (_heads(mq, H), _heads(mk, H) * (d ** -0.5), _heads(mv, H))
        log_i = [gates[..., (2 * j) * H:(2 * j + 1) * H].transpose(0, 2, 1) for j in (0, 1)]
        log_f = [jax.nn.log_sigmoid(gates[..., (2 * j + 1) * H:(2 * j + 2) * H]).transpose(0, 2, 1) for j in (0, 1)]
        return m, log_i, log_f, mo, (_heads(nq, H), _heads(nk, H), _heads(nv, H))

    m_c, li_c, lf_c, o_c, n_c = project(h_ctx)
    m_l, li_l, lf_l, o_l, n_l = project(h_lat)

    hs_c = 0.0
    hs_l = 0.0
    for j, fl in enumerate((_ident, _flip)):
        state0 = (jnp.zeros((B, H, d, d), jnp.float32), jnp.zeros((B, H, d), jnp.float32),
                  jnp.zeros((B, H), jnp.float32))
        y_c, state = mlstm_scan(*[fl(t) for t in m_c], fl(li_c[j]), fl(lf_c[j]), state0)
        y_l, _ = mlstm_scan(*[fl(t) for t in m_l], fl(li_l[j]), fl(lf_l[j]), state)
        hs_c = hs_c + fl(y_c)
        hs_l = hs_l + fl(y_l)

    g_h = g_head.reshape(H, 1, d)
    ml_out_l = jax.nn.sigmoid(o_l) * _merge(rms_norm(hs_l, g_h))
    ml_out_c = jax.nn.sigmoid(o_c) * _merge(rms_norm(hs_c, g_h))
    na_l = neighbourhood_attention(n_l[0], n_l[1], n_l[2], n_c[1], n_c[2], rpb)
    na_c = block_attention(n_c[0], n_c[1], n_c[2])
    y_lat = jnp.concatenate([ml_out_l, _merge(na_l)], axis=-1) @ w_out
    y_ctx = jnp.concatenate([ml_out_c, _merge(na_c)], axis=-1) @ w_out
    return y_lat, y_ctx


def mixer_cd(h_lat, h_ctx, w_in, g_q, g_k, decay_logit, g_head, w_out):
    H, Hk, d = GROUP_HEADS, GQA_KV_HEADS, HEAD_DIM
    B, S, _ = h_lat.shape
    cos, sin = axial_rope(S, d)

    def project(h, rotate):
        gq, gk, gv, rq, rk, rv, rg = _split(h @ w_in, CD_SIZES)
        q = rms_norm(_heads(gq, H), g_q)
        k = rms_norm(_heads(gk, Hk), g_k)
        r_q = _heads(rq, H)
        r_k = _heads(rk, H) * (d ** -0.5)
        if rotate:
            q, k, r_q, r_k = (apply_rope(t, cos, sin) for t in (q, k, r_q, r_k))
        return (q, k, _heads(gv, Hk)), (r_q, r_k, _heads(rv, H)), rg

    a_c, r_c, g_c = project(h_ctx, False)
    a_l, r_l, g_l = project(h_lat, True)

    log_gamma = jax.nn.log_sigmoid(decay_logit.astype(jnp.float32))
    rs_c = 0.0
    rs_l = 0.0
    for j, fl in enumerate((_ident, _flip)):
        s0 = jnp.zeros((B, H, d, d), jnp.float32)
        y_c, s_ctx = retention_scan(*[fl(t) for t in r_c], log_gamma[j], s0)
        y_l, _ = retention_scan(*[fl(t) for t in r_l], log_gamma[j], s_ctx)
        rs_c = rs_c + fl(y_c)
        rs_l = rs_l + fl(y_l)

    g_h = g_head.reshape(H, 1, d)
    ret_l = jax.nn.silu(g_l) * _merge(rms_norm(rs_l, g_h))
    ret_c = jax.nn.silu(g_c) * _merge(rms_norm(rs_c, g_h))
    k_all = jnp.concatenate([a_c[1], a_l[1]], axis=2)
    v_all = jnp.concatenate([a_c[2], a_l[2]], axis=2)
    att_l = block_attention(a_l[0], k_all, v_all)
    att_c = block_attention(a_c[0], a_c[1], a_c[2])
    y_lat = jnp.concatenate([_merge(att_l), ret_l], axis=-1) @ w_out
    y_ctx = jnp.concatenate([_merge(att_c), ret_c], axis=-1) @ w_out
    return y_lat, y_ctx


def moe_ffn(h, w_router, b_router, w_gate, w_up, w_down):
    T = h.shape[0]
    aff = jax.nn.sigmoid(h.astype(jnp.float32) @ w_router.astype(jnp.float32))
    sel = aff + b_router.astype(jnp.float32)
    grp_score = jnp.sum(lax.top_k(sel.reshape(T, N_GROUPS, EXPERTS_PER_GROUP), TOP_K)[0], axis=-1)
    in_grp = jnp.repeat(jax.nn.one_hot(jnp.argmax(grp_score, axis=-1), N_GROUPS, dtype=jnp.bool_),
                        EXPERTS_PER_GROUP, axis=-1)
    _, top_idx = lax.top_k(jnp.where(in_grp, sel, -jnp.inf), TOP_K)
    top_w = jnp.take_along_axis(aff, top_idx, axis=-1)
    top_w = top_w / jnp.sum(top_w, axis=-1, keepdims=True)
    gates = jnp.einsum('tk,tke->te', top_w, jax.nn.one_hot(top_idx, N_EXPERTS, dtype=jnp.float32)).astype(h.dtype)
    out = jnp.zeros_like(h)
    for e in range(N_EXPERTS):
        y = (jax.nn.silu(h @ w_gate[e]) * (h @ w_up[e])) @ w_down[e]
        out = out + gates[:, e:e + 1] * y
    return out


def setup_inputs(seed: int = 0) -> dict:
    key = jax.random.key(seed)
    ks = jax.random.split(key, 32)
    f32 = jnp.float32
    D = D_MODEL
    H = GROUP_HEADS
    n_even = (DEPTH + 1) // 2
    n_odd = DEPTH // 2

    def nrm(k, shape, scale):
        return jax.random.normal(k, shape, f32) * scale

    def gain(k, shape):
        return 1.0 + nrm(k, shape, 0.02)

    f_bias = jnp.asarray(np.linspace(3.0, 6.0, H, dtype=np.float32))
    zeros_h = jnp.zeros((H,), f32)
    gate_base = jnp.concatenate([zeros_h, f_bias, zeros_h, f_bias])
    gam = 1.0 - 2.0 ** (-5.0 - np.arange(H, dtype=np.float32))
    decay_base = jnp.asarray(np.log(gam / (1.0 - gam)).astype(np.float32))
    return {
        'x': nrm(ks[0], (BATCH, SEQ, D), 1.0),
        'c': nrm(ks[1], (BATCH, D), 1.0),
        'ctx': nrm(ks[2], (BATCH, CTX_LEN, D), 1.0),
        'c_ctx': nrm(ks[3], (D,), 1.0),
        'w_mod': nrm(ks[4], (DEPTH, D, N_MOD * D), 0.5 * D ** -0.5),
        'b_mod': nrm(ks[5], (DEPTH, N_MOD * D), 0.02),
        'g_norm_mix': gain(ks[6], (DEPTH, D)),
        'g_norm_ffn': gain(ks[7], (DEPTH, D)),
        'g_norm_out': gain(ks[8], (D,)),
        'w_in_ab': nrm(ks[9], (n_even, D, sum(AB_SIZES)), D ** -0.5),
        'b_gate_ab': gate_base[None] + nrm(ks[10], (n_even, 4 * H), 0.1),
        'g_mlstm': gain(ks[11], (n_even, GROUP_W)),
        'rpb_na': nrm(ks[12], (n_even, H, 2 * NA_WIN_R - 1, 2 * NA_WIN_C - 1), 0.1),
        'w_out_ab': nrm(ks[13], (n_even, 2 * GROUP_W, D), (2 * GROUP_W) ** -0.5),
        'w_in_cd': nrm(ks[14], (n_odd, D, sum(CD_SIZES)), D ** -0.5),
        'g_qnorm': gain(ks[15], (n_odd, HEAD_DIM)),
        'g_knorm': gain(ks[16], (n_odd, HEAD_DIM)),
        'ret_decay_logit': decay_base + nrm(ks[17], (n_odd, 2, H), 0.1),
        'g_ret': gain(ks[18], (n_odd, GROUP_W)),
        'w_out_cd': nrm(ks[19], (n_odd, 2 * GROUP_W, D), (2 * GROUP_W) ** -0.5),
        'w_router': nrm(ks[20], (D, N_EXPERTS), D ** -0.5),
        'b_router': nrm(ks[21], (N_EXPERTS,), 0.01),
        'w_exp_gate': nrm(ks[22], (DEPTH, N_EXPERTS, D, D_FF_EXPERT), D ** -0.5),
        'w_exp_up': nrm(ks[23], (DEPTH, N_EXPERTS, D, D_FF_EXPERT), D ** -0.5),
        'w_exp_down': nrm(ks[24], (DEPTH, N_EXPERTS, D_FF_EXPERT, D), D_FF_EXPERT ** -0.5),
    }


def reference(x, c, ctx, c_ctx, w_mod, b_mod, g_norm_mix, g_norm_ffn, g_norm_out,
              w_in_ab, b_gate_ab, g_mlstm, rpb_na, w_out_ab,
              w_in_cd, g_qnorm, g_knorm, ret_decay_logit, g_ret, w_out_cd,
              w_router, b_router, w_exp_gate, w_exp_up, w_exp_down):
    B, S, D = x.shape
    xc = ctx
    Lc = ctx.shape[1]
    for layer in range(DEPTH):
        mod = (jax.nn.silu(c) @ w_mod[layer] + b_mod[layer])[:, None, :]
        mod_c = (jax.nn.silu(c_ctx) @ w_mod[layer] + b_mod[layer])[None, None, :]
        sh1, sc1, gt1, sh2, sc2, gt2 = jnp.split(mod, N_MOD, axis=-1)
        sh1c, sc1c, gt1c, sh2c, sc2c, gt2c = jnp.split(mod_c, N_MOD, axis=-1)
        h = rms_norm(x, g_norm_mix[layer]) * (1.0 + sc1) + sh1
        hc = rms_norm(xc, g_norm_mix[layer]) * (1.0 + sc1c) + sh1c
        p = layer // 2
        if layer % 2 == 0:
            y, yc = mixer_ab(h, hc, w_in_ab[p], b_gate_ab[p], g_mlstm[p], rpb_na[p], w_out_ab[p])
        else:
            y, yc = mixer_cd(h, hc, w_in_cd[p], g_qnorm[p], g_knorm[p], ret_decay_logit[p], g_ret[p], w_out_cd[p])
        x = x + gt1 * y
        h = rms_norm(x, g_norm_ffn[layer]) * (1.0 + sc2) + sh2
        moe_w = (w_router, b_router, w_exp_gate[layer], w_exp_up[layer], w_exp_down[layer])
        if layer == DEPTH - 1:
            x = x + gt2 * moe_ffn(h.reshape(B * S, D), *moe_w).reshape(B, S, D)
        else:
            xc = xc + gt1c * yc
            hc = rms_norm(xc, g_norm_ffn[layer]) * (1.0 + sc2c) + sh2c
            f = moe_ffn(jnp.concatenate([h.reshape(B * S, D), hc.reshape(B * Lc, D)], axis=0), *moe_w)
            x = x + gt2 * f[:B * S].reshape(B, S, D)
            xc = xc + gt2c * f[B * S:].reshape(B, Lc, D)
    return rms_norm(x, g_norm_out)
```

```python
import functools

import jax
import jax.numpy as jnp
import numpy as np
from jax import lax
from jax.experimental import pallas as pl
from jax.experimental.pallas import tpu as pltpu

F32 = jnp.float32
BF16 = jnp.bfloat16
HIGHEST = lax.Precision.HIGHEST

HEAD_DIM = 128
GRID_W = 64
NA_WIN_R = 8
NA_WIN_C = 16
ROPE_BASE = 10000.0
N_EXPERTS = 16
N_GROUPS = 4
EXPERTS_PER_GROUP = N_EXPERTS // N_GROUPS
N_MOD = 6
EPS = 1e-6
Q_PER_KV = 4
SCAN_CHUNK = 256
LANES = 128
SUBLANES = 8
MOE_TILE = 256
COMBINE_TILE = 256
NEG_BIG = -1e30
VMEM_LIMIT = 56 * 1024 * 1024

_NT = (((1,), (1,)), ((), ()))
_TN = (((0,), (0,)), ((), ()))


def _cparams(sem):
    return pltpu.CompilerParams(dimension_semantics=sem, vmem_limit_bytes=VMEM_LIMIT)


def _sigmoid(x):
    return 1.0 / (1.0 + jnp.exp(-x))


def _log_sigmoid(x):
    return jnp.minimum(x, 0.0) - jnp.log(1.0 + jnp.exp(-jnp.abs(x)))


def _row_tile(p_rows):
    best = 256
    for t in (512, 768):
        if p_rows % t == 0:
            best = t
    return best


def _col_tile(n_cols):
    for t in (512, 256, LANES):
        if n_cols % t == 0:
            return t
    raise ValueError(f"projection width {n_cols} is not a multiple of {LANES}")


def _select_rows(ref, i, *, lc, tm, tpb, nb):
    b = i // tpb
    j = i % tpb
    per_sample = ref[pl.ds(b, 1), :]
    ctx = ref[nb:nb + 1, :]
    pos = j * tm + lax.broadcasted_iota(jnp.int32, (tm, 1), 0)
    return jnp.where(pos < lc, ctx, per_sample)


def _norm_mod(x, g, sh_ref, sc_ref, i, **kw):
    ms = jnp.mean(x * x, axis=-1, keepdims=True)
    y = x * lax.rsqrt(ms + EPS) * g
    return y * (1.0 + _select_rows(sc_ref, i, **kw)) + _select_rows(sh_ref, i, **kw)


def _rope(x, cosf, sinf):
    return x * cosf + pltpu.roll(x, HEAD_DIM // 2, 1) * sinf


def _mod_kernel(cc_ref, w_ref, b_ref, o_ref):
    a = cc_ref[...]
    a = a * _sigmoid(a)
    o_ref[0] = jnp.dot(a.astype(BF16), w_ref[0].astype(BF16), preferred_element_type=F32) + b_ref[0]


def _modulation(cc, w_mod, b_mod):
    depth, d, n = w_mod.shape
    mp = cc.shape[0]
    tn = 1024
    return pl.pallas_call(
        _mod_kernel,
        name="modulation",
        grid=(depth, n // tn),
        in_specs=[
            pl.BlockSpec((mp, d), lambda l, j: (0, 0)),
            pl.BlockSpec((1, d, tn), lambda l, j: (l, 0, j)),
            pl.BlockSpec((1, 1, tn), lambda l, j: (l, 0, j)),
        ],
        out_specs=pl.BlockSpec((1, mp, tn), lambda l, j: (l, 0, j)),
        out_shape=jax.ShapeDtypeStruct((depth, mp, n), F32),
        compiler_params=_cparams(("arbitrary", "arbitrary")),
    )(cc, w_mod, b_mod.reshape(depth, 1, n))


def _inproj_kernel(*refs, with_gates, sel):
    if with_gates:
        x_ref, g_ref, sh_ref, sc_ref, w_ref, wgt_ref, bg_ref, o_ref, gt_ref, h_scr = refs
    else:
        x_ref, g_ref, sh_ref, sc_ref, w_ref, o_ref, h_scr = refs
    i = pl.program_id(0)

    @pl.when(pl.program_id(1) == 0)
    def _():
        h = _norm_mod(x_ref[...], g_ref[...], sh_ref, sc_ref, i, **sel)
        h_scr[...] = h.astype(BF16)
        if with_gates:
            gt_ref[...] = lax.dot_general(wgt_ref[...], h_scr[...], _NT, preferred_element_type=F32) + bg_ref[...]

    o_ref[...] = jnp.dot(h_scr[...], w_ref[...], preferred_element_type=F32).astype(o_ref.dtype)


def _in_projection(xs, g, mods, w, wgt, bg, *, sel):
    t, d = xs.shape
    n = w.shape[1]
    tn = _col_tile(n)
    tm = sel["tm"]
    mp = mods.shape[0]
    with_gates = wgt is not None
    in_specs = [
        pl.BlockSpec((tm, d), lambda i, j: (i, 0)),
        pl.BlockSpec((1, d), lambda i, j: (0, 0)),
        pl.BlockSpec((mp, d), lambda i, j: (0, 0)),
        pl.BlockSpec((mp, d), lambda i, j: (0, 1)),
        pl.BlockSpec((d, tn), lambda i, j: (0, j)),
    ]
    out_specs = [pl.BlockSpec((tm, tn), lambda i, j: (i, j))]
    out_shape = [jax.ShapeDtypeStruct((t, n), BF16)]
    args = [xs, g.reshape(1, d), mods, mods, w]
    if with_gates:
        ng = wgt.shape[0]
        in_specs += [pl.BlockSpec((ng, d), lambda i, j: (0, 0)), pl.BlockSpec((ng, 1), lambda i, j: (0, 0))]
        out_specs.append(pl.BlockSpec((ng, tm), lambda i, j: (0, i)))
        out_shape.append(jax.ShapeDtypeStruct((ng, t), F32))
        args += [wgt, bg.reshape(ng, 1)]
    res = pl.pallas_call(
        functools.partial(_inproj_kernel, with_gates=with_gates, sel=sel),
        name="in_projection",
        grid=(t // tm, n // tn),
        in_specs=in_specs,
        out_specs=out_specs,
        out_shape=out_shape,
        scratch_shapes=[pltpu.VMEM((tm, d), BF16)],
        compiler_params=_cparams(("arbitrary", "arbitrary")),
    )(*args)
    return res if with_gates else res[0]


def _chunk_order(nc, ncc, reverse):
    ctx = list(range(ncc))
    lat = list(range(ncc, nc))
    return (ctx[::-1] + lat[::-1]) if reverse else (ctx + lat)


def _mlstm_kernel(q_ref, k_ref, v_ref, o_ref, gates_ref, gh_ref, out_ref, vaug, yacc, *, nc, ncc):
    L = SCAN_CHUNK
    d = HEAD_DIM
    scale = d ** -0.5
    p_rows = nc * L
    vaug[:, 0:d] = v_ref[...]
    lane = lax.broadcasted_iota(jnp.int32, (p_rows, d), 1)
    vaug[:, d:2 * d] = jnp.where(lane == 0, 1.0, 0.0).astype(BF16)
    gates = gates_ref[...]
    ii = lax.broadcasted_iota(jnp.int32, (L, L), 0)
    jj = lax.broadcasted_iota(jnp.int32, (L, L), 1)
    eye = (ii == jj).astype(F32)
    for dirn in (0, 1):
        li = gates[2 * dirn]
        lf = _log_sigmoid(gates[2 * dirn + 1])
        mask = (jj <= ii) if dirn == 0 else (jj >= ii)
        mask_f = mask.astype(F32)
        b_row = lax.dot_general(lf, mask_f, _NT, precision=HIGHEST, preferred_element_type=F32)
        b_col = lax.dot_general(mask_f, lf, _NT, precision=HIGHEST, preferred_element_type=F32)
        li_col = lax.dot_general(eye, li, _NT, precision=HIGHEST, preferred_element_type=F32)
        a_row = li - b_row
        a_col = li_col - b_col
        c_state = jnp.zeros((d, 2 * d), F32)
        m = jnp.zeros((1, 1), F32)
        for c in _chunk_order(nc, ncc, dirn == 1):
            sl = slice(c * L, (c + 1) * L)
            qc = q_ref[sl, :]
            kc = k_ref[sl, :]
            va = vaug[sl, :]
            bcol = b_col[:, c:c + 1]
            arow = a_row[c:c + 1, :]
            acol = a_col[:, c:c + 1]
            b_end = b_row[c:c + 1, L - 1:L] if dirn == 0 else b_row[c:c + 1, 0:1]
            dm = jnp.where(mask, bcol + arow, -jnp.inf)
            m_row = jnp.maximum(bcol + m, jnp.max(dm, axis=-1, keepdims=True))
            w = jnp.exp(dm - m_row) * scale
            w_inter = jnp.exp(bcol + m - m_row)
            qk = lax.dot_general(qc, kc, _NT, preferred_element_type=F32)
            s = (qk * w).astype(BF16)
            r = (jnp.dot(s, va, preferred_element_type=F32)
                 + w_inter * jnp.dot(qc, c_state.astype(BF16), preferred_element_type=F32))
            h = r[:, 0:d] / jnp.maximum(jnp.abs(r[:, d:d + 1]), jnp.exp(-m_row))
            if dirn == 0:
                yacc[sl, :] = h
            else:
                yacc[sl, :] = yacc[sl, :] + h
            m_new = jnp.maximum(b_end + m, jnp.max(b_end + arow, axis=-1, keepdims=True))
            w_end = jnp.exp(b_end + acol - m_new) * scale
            w_prev = jnp.exp(b_end + m - m_new)
            kw = (kc.astype(F32) * w_end).astype(BF16)
            c_state = w_prev * c_state + lax.dot_general(kw, va, _TN, preferred_element_type=F32)
            m = m_new
    y = yacc[...]
    yn = y * lax.rsqrt(jnp.mean(y * y, axis=-1, keepdims=True) + EPS) * gh_ref[...]
    out_ref[...] = (_sigmoid(o_ref[...].astype(F32)) * yn).astype(out_ref.dtype)


def _mlstm(proj, gates, g_head, *, nb, heads, p_rows, lc, width):
    nc = p_rows // SCAN_CHUNK
    ncc = lc // SCAN_CHUNK
    ncp = gates.shape[3]
    d = HEAD_DIM
    t = proj.shape[0]

    def col(base):
        return pl.BlockSpec((p_rows, d), lambda b, h: (b, base * heads + h))

    return pl.pallas_call(
        functools.partial(_mlstm_kernel, nc=nc, ncc=ncc),
        name="mlstm",
        grid=(nb, heads),
        in_specs=[col(0), col(1), col(2), col(3),
                  pl.BlockSpec((4, None, None, ncp, SCAN_CHUNK), lambda b, h: (0, h, b, 0, 0)),
                  pl.BlockSpec((None, 1, d), lambda b, h: (h, 0, 0))],
        out_specs=pl.BlockSpec((p_rows, d), lambda b, h: (b, h)),
        out_shape=jax.ShapeDtypeStruct((t, width), BF16),
        scratch_shapes=[pltpu.VMEM((p_rows, 2 * d), BF16), pltpu.VMEM((p_rows, d), F32)],
        compiler_params=_cparams(("arbitrary", "arbitrary")),
    )(proj, proj, proj, proj, gates, g_head.reshape(heads, 1, d))


def _softmax_pv(parts):
    m = parts[0][0].max(axis=-1, keepdims=True)
    for s, _ in parts[1:]:
        m = jnp.maximum(m, s.max(axis=-1, keepdims=True))
    num = 0.0
    den = 0.0
    for s, v in parts:
        p = jnp.exp(s - m)
        den = den + jnp.sum(p, axis=-1, keepdims=True)
        num = num + jnp.dot(p.astype(BF16), v, preferred_element_type=F32)
    return num / den


def _natten_kernel(q_ref, k_ref, v_ref, bias_ref, out_ref, *, lc, rows):
    scale = HEAD_DIM ** -0.5
    wr = NA_WIN_R
    kc = k_ref[0:lc, :]
    vc = v_ref[0:lc, :]
    s_cc = lax.dot_general(q_ref[0:lc, :], kc, _NT, preferred_element_type=F32) * scale
    out_ref[0:lc, :] = _softmax_pv([(s_cc, vc)]).astype(out_ref.dtype)

    def body(r, carry):
        r0 = jnp.clip(r - wr // 2, 0, rows - wr)
        d0 = r0 - r + NA_WIN_R - 1
        q0 = pl.multiple_of(lc + r * GRID_W, GRID_W)
        k0 = pl.multiple_of(lc + r0 * GRID_W, GRID_W)
        qr = q_ref[pl.ds(q0, GRID_W), :]
        kb = k_ref[pl.ds(k0, wr * GRID_W), :]
        vb = v_ref[pl.ds(k0, wr * GRID_W), :]
        s_lat = lax.dot_general(qr, kb, _NT, preferred_element_type=F32) * scale + bias_ref[d0]
        s_ctx = lax.dot_general(qr, kc, _NT, preferred_element_type=F32) * scale
        out_ref[pl.ds(q0, GRID_W), :] = _softmax_pv([(s_lat, vb), (s_ctx, vc)]).astype(out_ref.dtype)
        return carry

    lax.fori_loop(0, rows, body, 0)


def _natten_bias(rpb, rows):
    assert rows >= NA_WIN_R
    wr = NA_WIN_R
    col = np.arange(GRID_W)
    c0 = np.clip(col - NA_WIN_C // 2, 0, GRID_W - NA_WIN_C)
    col_ok = (col[None, :] >= c0[:, None]) & (col[None, :] < c0[:, None] + NA_WIN_C)
    dc_idx = np.clip(col[None, :] - col[:, None], -(NA_WIN_C - 1), NA_WIN_C - 1) + NA_WIN_C - 1
    bias_c = jnp.where(col_ok[None, None], rpb[:, :, dc_idx], NEG_BIG)
    return jnp.stack([jnp.concatenate([bias_c[:, d0 + j] for j in range(wr)], axis=-1)
                      for d0 in range(NA_WIN_R)], axis=1)


def _natten(proj, bias, *, nb, heads, p_rows, lc, width, base):
    d = HEAD_DIM
    t = proj.shape[0]
    rows = (p_rows - lc) // GRID_W

    def col(k):
        return pl.BlockSpec((p_rows, d), lambda b, h: (b, (base + k) * heads + h))

    return pl.pallas_call(
        functools.partial(_natten_kernel, lc=lc, rows=rows),
        name="natten",
        grid=(nb, heads),
        in_specs=[col(0), col(1), col(2),
                  pl.BlockSpec((None,) + bias.shape[1:], lambda b, h: (h, 0, 0, 0))],
        out_specs=pl.BlockSpec((p_rows, d), lambda b, h: (b, h)),
        out_shape=jax.ShapeDtypeStruct((t, width), BF16),
        compiler_params=_cparams(("arbitrary", "arbitrary")),
    )(proj, proj, proj, bias)


def _head_norm(x, g):
    return x * lax.rsqrt(jnp.mean(x * x, axis=-1, keepdims=True) + EPS) * g


def _gqa_kernel(q_ref, k_ref, v_ref, gq_ref, gk_ref, cos_ref, sin_ref, out_ref, kn_scr, *, lc, tq):
    d = HEAD_DIM
    scale = d ** -0.5
    qt = pl.program_id(2)
    ncc = lc // tq
    p_rows = kn_scr.shape[0]

    @pl.when(qt == 0)
    def _():
        kn = _head_norm(k_ref[...].astype(F32), gk_ref[...])
        kn_scr[0:lc, :] = kn[0:lc].astype(BF16)
        kn_scr[lc:, :] = _rope(kn[lc:], cos_ref[...], sin_ref[...]).astype(BF16)

    is_lat = qt >= ncc
    off = pl.multiple_of(jnp.maximum(qt - ncc, 0) * tq, tq)
    cosf = cos_ref[pl.ds(off, tq), :]
    sinf = sin_ref[pl.ds(off, tq), :]
    kpos = lax.broadcasted_iota(jnp.int32, (1, p_rows), 1)
    key_ok = jnp.logical_or(is_lat, kpos < lc)
    for g in range(Q_PER_KV):
        qn = _head_norm(q_ref[:, g * d:(g + 1) * d].astype(F32), gq_ref[...])
        qn = jnp.where(is_lat, _rope(qn, cosf, sinf), qn) * scale
        s = lax.dot_general(qn.astype(BF16), kn_scr[...], _NT, preferred_element_type=F32)
        s = jnp.where(key_ok, s, NEG_BIG)
        out_ref[:, g * d:(g + 1) * d] = _softmax_pv([(s, v_ref[...])]).astype(out_ref.dtype)


def _gqa(proj, g_q, g_k, cosf, sinf, *, nb, heads, p_rows, lc, width):
    d = HEAD_DIM
    kvh = heads // Q_PER_KV
    t = proj.shape[0]
    tq = 256
    nq = p_rows // tq
    s_rows = cosf.shape[0]
    gw = Q_PER_KV * d
    return pl.pallas_call(
        functools.partial(_gqa_kernel, lc=lc, tq=tq),
        name="gqa",
        grid=(nb, kvh, nq),
        in_specs=[pl.BlockSpec((tq, gw), lambda b, kh, i: (b * nq + i, kh)),
                  pl.BlockSpec((p_rows, d), lambda b, kh, i: (b, heads + kh)),
                  pl.BlockSpec((p_rows, d), lambda b, kh, i: (b, heads + kvh + kh)),
                  pl.BlockSpec((1, d), lambda b, kh, i: (0, 0)),
                  pl.BlockSpec((1, d), lambda b, kh, i: (0, 0)),
                  pl.BlockSpec((s_rows, d), lambda b, kh, i: (0, 0)),
                  pl.BlockSpec((s_rows, d), lambda b, kh, i: (0, 0))],
        out_specs=pl.BlockSpec((tq, gw), lambda b, kh, i: (b * nq + i, kh)),
        out_shape=jax.ShapeDtypeStruct((t, width), BF16),
        scratch_shapes=[pltpu.VMEM((p_rows, d), BF16)],
        compiler_params=_cparams(("arbitrary", "arbitrary", "arbitrary")),
    )(proj, proj, proj, g_q.reshape(1, d), g_k.reshape(1, d), cosf, sinf)


def _retention_kernel(q_ref, k_ref, v_ref, g_ref, lg_ref, gh_ref, cos_ref, sin_ref, out_ref,
                      qs, ks, yacc, *, nc, ncc, lc):
    L = SCAN_CHUNK
    d = HEAD_DIM
    scale = d ** -0.5
    cosf = cos_ref[...]
    sinf = sin_ref[...]
    qs[0:lc, :] = q_ref[0:lc, :]
    ks[0:lc, :] = k_ref[0:lc, :]
    qs[lc:, :] = _rope(q_ref[lc:, :].astype(F32), cosf, sinf).astype(BF16)
    ks[lc:, :] = _rope(k_ref[lc:, :].astype(F32), cosf, sinf).astype(BF16)
    ii = lax.broadcasted_iota(jnp.int32, (L, L), 0)
    jj = lax.broadcasted_iota(jnp.int32, (L, L), 1)
    pos = lax.broadcasted_iota(jnp.int32, (L, 1), 0).astype(F32)
    for dirn in (0, 1):
        lg = _log_sigmoid(lg_ref[dirn:dirn + 1, 0:1])
        rel = (ii - jj) if dirn == 0 else (jj - ii)
        decay = jnp.where(rel >= 0, jnp.exp(jnp.maximum(rel, 0).astype(F32) * lg), 0.0) * scale
        if dirn == 0:
            dq = jnp.exp((pos + 1.0) * lg)
            dk = jnp.exp((L - 1.0 - pos) * lg) * scale
        else:
            dq = jnp.exp((L - pos) * lg)
            dk = jnp.exp(pos * lg) * scale
        dchunk = jnp.exp(L * lg)
        state = jnp.zeros((d, d), F32)
        for c in _chunk_order(nc, ncc, dirn == 1):
            sl = slice(c * L, (c + 1) * L)
            qc = qs[sl, :]
            kc = ks[sl, :]
            vc = v_ref[sl, :]
            s = (lax.dot_general(qc, kc, _NT, preferred_element_type=F32) * decay).astype(BF16)
            o = (jnp.dot(s, vc, preferred_element_type=F32)
                 + dq * jnp.dot(qc, state.astype(BF16), preferred_element_type=F32))
            if dirn == 0:
                yacc[sl, :] = o
            else:
                yacc[sl, :] = yacc[sl, :] + o
            kw = (kc.astype(F32) * dk).astype(BF16)
            state = dchunk * state + lax.dot_general(kw, vc, _TN, preferred_element_type=F32)
    y = yacc[...]
    yn = y * lax.rsqrt(jnp.mean(y * y, axis=-1, keepdims=True) + EPS) * gh_ref[...]
    gate = g_ref[...].astype(F32)
    out_ref[...] = (gate * _sigmoid(gate) * yn).astype(out_ref.dtype)


def _retention(proj, decay_logit, g_head, cosf, sinf, *, nb, heads, p_rows, lc, width, base):
    d = HEAD_DIM
    nc = p_rows // SCAN_CHUNK
    ncc = lc // SCAN_CHUNK
    t = proj.shape[0]
    s_rows = cosf.shape[0]
    lg = jnp.broadcast_to(decay_logit.T[:, :, None], (heads, 2, d)).astype(F32)
    lg = jnp.concatenate([lg, jnp.zeros((heads, SUBLANES - 2, d), F32)], axis=1)

    def col(k):
        return pl.BlockSpec((p_rows, d), lambda b, h: (b, base + k * heads + h))

    return pl.pallas_call(
        functools.partial(_retention_kernel, nc=nc, ncc=ncc, lc=lc),
        name="retention",
        grid=(nb, heads),
        in_specs=[col(0), col(1), col(2), col(3),
                  pl.BlockSpec((None, SUBLANES, d), lambda b, h: (h, 0, 0)),
                  pl.BlockSpec((None, 1, d), lambda b, h: (h, 0, 0)),
                  pl.BlockSpec((s_rows, d), lambda b, h: (0, 0)),
                  pl.BlockSpec((s_rows, d), lambda b, h: (0, 0))],
        out_specs=pl.BlockSpec((p_rows, d), lambda b, h: (b, h)),
        out_shape=jax.ShapeDtypeStruct((t, width), BF16),
        scratch_shapes=[pltpu.VMEM((p_rows, d), BF16), pltpu.VMEM((p_rows, d), BF16),
                        pltpu.VMEM((p_rows, d), F32)],
        compiler_params=_cparams(("arbitrary", "arbitrary")),
    )(proj, proj, proj, proj, lg, g_head.reshape(heads, 1, d), cosf, sinf)


def _outproj_kernel(a_ref, b_ref, wa_ref, wb_ref, x_ref, gt_ref, o_ref, *, sel):
    i = pl.program_id(0)
    y = (jnp.dot(a_ref[...], wa_ref[...], preferred_element_type=F32)
         + jnp.dot(b_ref[...], wb_ref[...], preferred_element_type=F32))
    o_ref[...] = x_ref[...] + _select_rows(gt_ref, i, **sel) * y


def _out_projection(mix_a, mix_b, w_a, w_b, xs, mods, *, sel, gate_chunk):
    t, d = xs.shape
    gw = mix_a.shape[1]
    tm = sel["tm"]
    tn = min(1024, d)
    mp = mods.shape[0]
    npd = d // tn
    return pl.pallas_call(
        functools.partial(_outproj_kernel, sel=sel),
        name="out_projection",
        grid=(t // tm, npd),
        in_specs=[pl.BlockSpec((tm, gw), lambda i, j: (i, 0)),
                  pl.BlockSpec((tm, gw), lambda i, j: (i, 0)),
                  pl.BlockSpec((gw, tn), lambda i, j: (0, j)),
                  pl.BlockSpec((gw, tn), lambda i, j: (0, j)),
                  pl.BlockSpec((tm, tn), lambda i, j: (i, j)),
                  pl.BlockSpec((mp, tn), lambda i, j: (0, gate_chunk * npd + j))],
        out_specs=pl.BlockSpec((tm, tn), lambda i, j: (i, j)),
        out_shape=jax.ShapeDtypeStruct((t, d), F32),
        compiler_params=_cparams(("arbitrary", "arbitrary")),
    )(mix_a, mix_b, w_a, w_b, xs, mods)


def _first_argmax(vals):
    best = vals[0]
    idx = jnp.zeros_like(best)
    for j in range(1, len(vals)):
        upd = vals[j] > best
        idx = jnp.where(upd, float(j), idx)
        best = jnp.where(upd, vals[j], best)
    return idx, best


def _pick(idx, vals):
    out = vals[-1]
    for j in range(len(vals) - 2, -1, -1):
        out = jnp.where(idx == float(j), vals[j], out)
    return out


def _ffn_pre_kernel(x_ref, g_ref, sh_ref, sc_ref, wrt_ref, br_ref, h_ref, r_ref, *, sel):
    i = pl.program_id(0)
    h = _norm_mod(x_ref[...], g_ref[...], sh_ref, sc_ref, i, **sel)
    h_ref[...] = h
    logits = lax.dot_general(wrt_ref[...], h, _NT, precision=HIGHEST, preferred_element_type=F32)
    aff = _sigmoid(logits)
    sel_s = aff + br_ref[...]
    a = [aff[e:e + 1, :] for e in range(N_EXPERTS)]
    s = [sel_s[e:e + 1, :] for e in range(N_EXPERTS)]
    n = EXPERTS_PER_GROUP
    scores = []
    for grp in range(N_GROUPS):
        v = s[grp * n:(grp + 1) * n]
        best = v[0] + v[1]
        for p in range(n):
            for q in range(p + 1, n):
                if (p, q) != (0, 1):
                    best = jnp.maximum(best, v[p] + v[q])
        scores.append(best)
    gi, _ = _first_argmax(scores)
    cv = [_pick(gi, [s[grp * n + j] for grp in range(N_GROUPS)]) for j in range(n)]
    av = [_pick(gi, [a[grp * n + j] for grp in range(N_GROUPS)]) for j in range(n)]
    i1, _ = _first_argmax(cv)
    cv2 = [jnp.where(i1 == float(j), -jnp.inf, cv[j]) for j in range(n)]
    i2, _ = _first_argmax(cv2)
    w1 = _pick(i1, av)
    w2 = _pick(i2, av)
    tot = w1 + w2
    r_ref[0:1, :] = gi * float(n) + i1
    r_ref[1:2, :] = gi * float(n) + i2
    r_ref[2:3, :] = w1 / tot
    r_ref[3:4, :] = w2 / tot
    r_ref[4:8, :] = jnp.zeros((4, r_ref.shape[1]), F32)


def _ffn_pre(xs, g, mods, w_router_t, b_router, *, sel):
    t, d = xs.shape
    tm = sel["tm"]
    mp = mods.shape[0]
    ne = w_router_t.shape[0]
    return pl.pallas_call(
        functools.partial(_ffn_pre_kernel, sel=sel),
        name="ffn_pre_router",
        grid=(t // tm,),
        in_specs=[pl.BlockSpec((tm, d), lambda i: (i, 0)),
                  pl.BlockSpec((1, d), lambda i: (0, 0)),
                  pl.BlockSpec((mp, d), lambda i: (0, 3)),
                  pl.BlockSpec((mp, d), lambda i: (0, 4)),
                  pl.BlockSpec((ne, d), lambda i: (0, 0)),
                  pl.BlockSpec((ne, 1), lambda i: (0, 0))],
        out_specs=[pl.BlockSpec((tm, d), lambda i: (i, 0)),
                   pl.BlockSpec((SUBLANES, tm), lambda i: (0, i))],
        out_shape=[jax.ShapeDtypeStruct((t, d), F32), jax.ShapeDtypeStruct((SUBLANES, t), F32)],
        compiler_params=_cparams(("arbitrary",)),
    )(xs, g.reshape(1, d), mods, mods, w_router_t, b_router.reshape(ne, 1))


def _dispatch_metadata(route, tm):
    t = route.shape[1]
    na = 2 * t
    e = route[0:2].astype(jnp.int32).reshape(na)
    w = route[2:4].reshape(na)
    ne = N_EXPERTS
    order = jnp.argsort(e, stable=True)
    counts = jnp.sum((e[:, None] == jnp.arange(ne)[None, :]).astype(jnp.int32), axis=0)
    starts = jnp.cumsum(counts) - counts
    padded = ((counts + tm - 1) // tm) * tm
    ends = jnp.cumsum(padded)
    offs = ends - padded
    n_tiles = na // tm + ne
    n_slots = n_tiles * tm
    tile_start = jnp.arange(n_tiles, dtype=jnp.int32) * tm
    tile_valid = tile_start < ends[-1]
    tile_e = jnp.sum((tile_start[:, None] >= ends[None, :]).astype(jnp.int32), axis=1)
    last_e = jnp.max(jnp.where(tile_valid, tile_e, 0))
    tile_e = jnp.where(tile_valid, tile_e, last_e)
    slot = jnp.arange(n_slots, dtype=jnp.int32)
    se = tile_e[slot // tm]
    rank = slot - offs[se]
    valid = tile_valid[slot // tm] & (rank < counts[se])
    a = order[jnp.clip(starts[se] + rank, 0, na - 1)]
    tok = jnp.where(valid, a % t, 0).astype(jnp.int32)
    w_slot = jnp.where(valid, w[a], 0.0).reshape(n_slots, 1)
    inv = jnp.zeros((na,), jnp.int32).at[order].set(jnp.arange(na, dtype=jnp.int32))
    pos = (offs[e] + inv - starts[e]).astype(jnp.int32)
    return tile_e.astype(jnp.int32), tile_valid.astype(jnp.int32), tok, w_slot, pos


def _row_copy(src_hbm, row, dst, dst_row, sem):
    return pltpu.make_async_copy(src_hbm.at[pl.ds(row, 1), :], dst.at[pl.ds(dst_row, 1), :], sem)


def _moe_kernel(te_ref, tv_ref, tok_ref, h_hbm, ws_ref, wg_ref, wu_ref, wd_ref, y_ref, xbuf, sem, *, tm, nt):
    i = pl.program_id(0)
    slot = i % 2

    def start_tile(tile, s):
        def body(r, carry):
            _row_copy(h_hbm, tok_ref[tile * tm + r], xbuf.at[s], r, sem.at[s]).start()
            return carry
        lax.fori_loop(0, tm, body, 0, unroll=8)

    def wait_tile(s):
        def body(r, carry):
            _row_copy(h_hbm, 0, xbuf.at[s], r, sem.at[s]).wait()
            return carry
        lax.fori_loop(0, tm, body, 0, unroll=8)

    @pl.when(i == 0)
    def _():
        start_tile(0, 0)

    nxt = jnp.minimum(i + 1, nt - 1)

    @pl.when(jnp.logical_and(i + 1 < nt, tv_ref[nxt] == 1))
    def _():
        start_tile(i + 1, 1 - slot)

    @pl.when(tv_ref[i] == 1)
    def _():
        wait_tile(slot)
        xb = xbuf[slot].astype(BF16)
        g = jnp.dot(xb, wg_ref[...], preferred_element_type=F32)
        u = jnp.dot(xb, wu_ref[...], preferred_element_type=F32)
        act = (g * _sigmoid(g) * u).astype(BF16)
        y_ref[...] = jnp.dot(act, wd_ref[...], preferred_element_type=F32) * ws_ref[...]

    @pl.when(tv_ref[i] == 0)
    def _():
        y_ref[...] = jnp.zeros(y_ref.shape, y_ref.dtype)


def _moe_experts(h2, tile_e, tile_valid, tok, w_slot, wg, wu, wd):
    t, d = h2.shape
    tm = MOE_TILE
    n_slots = w_slot.shape[0]
    nt = n_slots // tm
    ff = wg.shape[2]
    grid_spec = pltpu.PrefetchScalarGridSpec(
        num_scalar_prefetch=3,
        grid=(nt,),
        in_specs=[pl.BlockSpec(memory_space=pl.ANY),
                  pl.BlockSpec((tm, 1), lambda i, te, tv, tk: (i, 0)),
                  pl.BlockSpec((None, d, ff), lambda i, te, tv, tk: (te[i], 0, 0)),
                  pl.BlockSpec((None, d, ff), lambda i, te, tv, tk: (te[i], 0, 0)),
                  pl.BlockSpec((None, ff, d), lambda i, te, tv, tk: (te[i], 0, 0))],
        out_specs=pl.BlockSpec((tm, d), lambda i, te, tv, tk: (i, 0)),
        scratch_shapes=[pltpu.VMEM((2, tm, d), F32), pltpu.SemaphoreType.DMA((2,))],
    )
    return pl.pallas_call(
        functools.partial(_moe_kernel, tm=tm, nt=nt),
        name="moe_experts",
        grid_spec=grid_spec,
        out_shape=jax.ShapeDtypeStruct((n_slots, d), F32),
        compiler_params=_cparams(("arbitrary",)),
    )(tile_e, tile_valid, tok, h2, w_slot, wg, wu, wd)


def _combine_kernel(pos_ref, y_hbm, x_ref, gt_ref, o_ref, ybuf, sem, *, tm, nt, t_total, sel):
    i = pl.program_id(0)
    slot = i % 2

    def start_tile(tile, s):
        def body(r, carry):
            for k in range(2):
                _row_copy(y_hbm, pos_ref[k * t_total + tile * tm + r], ybuf.at[s, k], r, sem.at[s]).start()
            return carry
        lax.fori_loop(0, tm, body, 0, unroll=4)

    def wait_tile(s):
        def body(r, carry):
            for k in range(2):
                _row_copy(y_hbm, 0, ybuf.at[s, k], r, sem.at[s]).wait()
            return carry
        lax.fori_loop(0, tm, body, 0, unroll=4)

    @pl.when(i == 0)
    def _():
        start_tile(0, 0)

    @pl.when(i + 1 < nt)
    def _():
        start_tile(i + 1, 1 - slot)

    wait_tile(slot)
    o_ref[...] = x_ref[...] + _select_rows(gt_ref, i, **sel) * (ybuf[slot, 0] + ybuf[slot, 1])


def _moe_combine(ys, pos, xs, mods, *, sel):
    t, d = xs.shape
    tm = sel["tm"]
    nt = t // tm
    mp = mods.shape[0]
    grid_spec = pltpu.PrefetchScalarGridSpec(
        num_scalar_prefetch=1,
        grid=(nt,),
        in_specs=[pl.BlockSpec(memory_space=pl.ANY),
                  pl.BlockSpec((tm, d), lambda i, ps: (i, 0)),
                  pl.BlockSpec((mp, d), lambda i, ps: (0, 5))],
        out_specs=pl.BlockSpec((tm, d), lambda i, ps: (i, 0)),
        scratch_shapes=[pltpu.VMEM((2, 2, tm, d), F32), pltpu.SemaphoreType.DMA((2,))],
    )
    return pl.pallas_call(
        functools.partial(_combine_kernel, tm=tm, nt=nt, t_total=t, sel=sel),
        name="moe_combine",
        grid_spec=grid_spec,
        out_shape=jax.ShapeDtypeStruct((t, d), F32),
        compiler_params=_cparams(("arbitrary",)),
    )(pos, ys, xs, mods)


def _final_norm_kernel(x_ref, g_ref, o_ref):
    x = x_ref[...]
    o_ref[...] = x * lax.rsqrt(jnp.mean(x * x, axis=-1, keepdims=True) + EPS) * g_ref[...]


def _final_norm(xs, g, *, nb, p_rows, lc):
    t, d = xs.shape
    tm = 256
    tpb = p_rows // tm
    ncc = lc // tm
    lat = tpb - ncc
    return pl.pallas_call(
        _final_norm_kernel,
        name="final_norm",
        grid=(nb, lat),
        in_specs=[pl.BlockSpec((tm, d), lambda b, j: (b * tpb + ncc + j, 0)),
                  pl.BlockSpec((1, d), lambda b, j: (0, 0))],
        out_specs=pl.BlockSpec((tm, d), lambda b, j: (b * lat + j, 0)),
        out_shape=jax.ShapeDtypeStruct((nb * lat * tm, d), F32),
        compiler_params=_cparams(("arbitrary", "arbitrary")),
    )(xs, g.reshape(1, d))


def _rope_tables(n_tokens):
    tpos = np.arange(n_tokens)
    row = (tpos // GRID_W).astype(np.float32)
    col = (tpos % GRID_W).astype(np.float32)
    n_axis = HEAD_DIM // 4
    inv = jnp.asarray(ROPE_BASE, F32) ** (-jnp.arange(n_axis, dtype=F32) / n_axis)
    ang = jnp.concatenate([jnp.asarray(row)[:, None] * inv, jnp.asarray(col)[:, None] * inv], axis=-1)
    cos, sin = jnp.cos(ang), jnp.sin(ang)
    return jnp.concatenate([cos, cos], axis=-1), jnp.concatenate([-sin, sin], axis=-1)


def kernel(x, c, ctx, c_ctx, w_mod, b_mod, g_norm_mix, g_norm_ffn, g_norm_out, w_in_ab, b_gate_ab, g_mlstm, rpb_na, w_out_ab, w_in_cd, g_qnorm, g_knorm, ret_decay_logit, g_ret, w_out_cd, w_router, b_router, w_exp_gate, w_exp_up, w_exp_down):
    nb, s_len, d = x.shape
    lc = ctx.shape[1]
    p_rows = lc + s_len
    heads = d // (2 * HEAD_DIM)
    kvh = heads // Q_PER_KV
    gw = heads * HEAD_DIM
    depth = w_mod.shape[0]
    assert lc % SCAN_CHUNK == 0 and s_len % SCAN_CHUNK == 0 and heads % Q_PER_KV == 0

    xs = jnp.concatenate([ctx, x], axis=1).reshape(nb * p_rows, d)
    mp = -(-(nb + 1) // SUBLANES) * SUBLANES
    cc = jnp.concatenate([c, c_ctx[None, :], jnp.zeros((mp - nb - 1, d), F32)], axis=0)
    mods = _modulation(cc, w_mod, b_mod)

    tm = _row_tile(p_rows)
    sel = dict(lc=lc, tm=tm, tpb=p_rows // tm, nb=nb)
    sel_c = dict(lc=lc, tm=COMBINE_TILE, tpb=p_rows // COMBINE_TILE, nb=nb)
    cosf, sinf = _rope_tables(s_len)
    nc = p_rows // SCAN_CHUNK
    ncp = -(-nc // SUBLANES) * SUBLANES
    seq = dict(nb=nb, heads=heads, p_rows=p_rows, lc=lc, width=gw)
    w_router_t = w_router.T.astype(F32)

    for layer in range(depth):
        ml = mods[layer]
        p = layer // 2
        if layer % 2 == 0:
            w = w_in_ab[p]
            ng = 4 * heads
            w_main = jnp.concatenate([w[:, :4 * gw], w[:, 4 * gw + ng:]], axis=1).astype(BF16)
            w_gate_t = w[:, 4 * gw:4 * gw + ng].T.astype(BF16)
            proj, gates_t = _in_projection(xs, g_norm_mix[layer], ml, w_main, w_gate_t, b_gate_ab[p], sel=sel)
            gates = gates_t.reshape(4, heads, nb, nc, SCAN_CHUNK)
            gates = jnp.pad(gates, ((0, 0), (0, 0), (0, 0), (0, ncp - nc), (0, 0)))
            mix_a = _mlstm(proj, gates, g_mlstm[p], **seq)
            mix_b = _natten(proj, _natten_bias(rpb_na[p], s_len // GRID_W), base=4, **seq)
            w_out = w_out_ab[p]
        else:
            proj = _in_projection(xs, g_norm_mix[layer], ml, w_in_cd[p].astype(BF16), None, None, sel=sel)
            mix_a = _gqa(proj, g_qnorm[p], g_knorm[p], cosf, sinf, **seq)
            mix_b = _retention(proj, ret_decay_logit[p], g_ret[p], cosf, sinf, base=heads + 2 * kvh, **seq)
            w_out = w_out_cd[p]
        xs = _out_projection(mix_a, mix_b, w_out[:gw].astype(BF16), w_out[gw:].astype(BF16), xs, ml,
                             sel=sel, gate_chunk=2)
        h2, route = _ffn_pre(xs, g_norm_ffn[layer], ml, w_router_t, b_router, sel=sel)
        tile_e, tile_valid, tok, w_slot, pos = _dispatch_metadata(route, MOE_TILE)
        ys = _moe_experts(h2, tile_e, tile_valid, tok, w_slot, w_exp_gate[layer].astype(BF16),
                          w_exp_up[layer].astype(BF16), w_exp_down[layer].astype(BF16))
        xs = _moe_combine(ys, pos, xs, ml, sel=sel_c)
    return _final_norm(xs, g_norm_out, nb=nb, p_rows=p_rows, lc=lc).reshape(nb, s_len, d)
```

```python
import functools

import jax
import jax.numpy as jnp
import numpy as np
from jax import lax
from jax.experimental import pallas as pl
from jax.experimental.pallas import tpu as pltpu

F32 = jnp.float32
BF16 = jnp.bfloat16
HIGHEST = lax.Precision.HIGHEST

HEAD_DIM = 128
GRID_W = 64
NA_WIN_R = 8
NA_WIN_C = 16
ROPE_BASE = 10000.0
N_EXPERTS = 16
N_GROUPS = 4
EXPERTS_PER_GROUP = N_EXPERTS // N_GROUPS
N_MOD = 6
EPS = 1e-6
Q_PER_KV = 4
SCAN_CHUNK = 256
LANES = 128
SUBLANES = 8
MOE_TILE = 256
MAX_COL_TILE = 1536
COMBINE_TILE = 256
NA_Q_ROWS = 4
NA_K_ROWS = NA_Q_ROWS + NA_WIN_R
NEG_BIG = -1e30
VMEM_LIMIT = 56 * 1024 * 1024

_NT = (((1,), (1,)), ((), ()))
_TN = (((0,), (0,)), ((), ()))


def _cparams(sem):
    return pltpu.CompilerParams(dimension_semantics=sem, vmem_limit_bytes=VMEM_LIMIT)


def _sigmoid(x):
    return 1.0 / (1.0 + jnp.exp(-x))


def _log_sigmoid(x):
    return jnp.minimum(x, 0.0) - jnp.log(1.0 + jnp.exp(-jnp.abs(x)))


def _row_tile(p_rows):
    best = 256
    for t in (512, 768):
        if p_rows % t == 0:
            best = t
    return best


def _col_tile(n_cols):
    assert n_cols % LANES == 0
    return max(t for t in range(LANES, MAX_COL_TILE + 1, LANES) if n_cols % t == 0)


def _select_rows(ref, i, *, lc, tm, tpb, nb):
    b = i // tpb
    j = i % tpb
    per_sample = ref[pl.ds(b, 1), :]
    ctx = ref[nb:nb + 1, :]
    pos = j * tm + lax.broadcasted_iota(jnp.int32, (tm, 1), 0)
    return jnp.where(pos < lc, ctx, per_sample)


def _norm_mod(x, g, sh_ref, sc_ref, i, **kw):
    ms = jnp.mean(x * x, axis=-1, keepdims=True)
    y = x * lax.rsqrt(ms + EPS) * g
    return y * (1.0 + _select_rows(sc_ref, i, **kw)) + _select_rows(sh_ref, i, **kw)


def _rope(x, cosf, sinf):
    return x * cosf + pltpu.roll(x, HEAD_DIM // 2, 1) * sinf


def _mod_kernel(cc_ref, w_ref, b_ref, o_ref):
    a = cc_ref[...]
    a = a * _sigmoid(a)
    o_ref[0] = jnp.dot(a.astype(BF16), w_ref[0].astype(BF16), preferred_element_type=F32) + b_ref[0]


def _modulation(cc, w_mod, b_mod):
    depth, d, n = w_mod.shape
    mp = cc.shape[0]
    tn = 1024
    return pl.pallas_call(
        _mod_kernel,
        name="modulation",
        grid=(depth, n // tn),
        in_specs=[
            pl.BlockSpec((mp, d), lambda l, j: (0, 0)),
            pl.BlockSpec((1, d, tn), lambda l, j: (l, 0, j)),
            pl.BlockSpec((1, 1, tn), lambda l, j: (l, 0, j)),
        ],
        out_specs=pl.BlockSpec((1, mp, tn), lambda l, j: (l, 0, j)),
        out_shape=jax.ShapeDtypeStruct((depth, mp, n), F32),
        compiler_params=_cparams(("arbitrary", "arbitrary")),
    )(cc, w_mod, b_mod.reshape(depth, 1, n))


def _inproj_kernel(*refs, with_gates, sel):
    if with_gates:
        x_ref, g_ref, sh_ref, sc_ref, w_ref, wgt_ref, bg_ref, o_ref, gt_ref, h_scr = refs
    else:
        x_ref, g_ref, sh_ref, sc_ref, w_ref, o_ref, h_scr = refs
    i = pl.program_id(0)

    @pl.when(pl.program_id(1) == 0)
    def _():
        h = _norm_mod(x_ref[...], g_ref[...], sh_ref, sc_ref, i, **sel)
        h_scr[...] = h.astype(BF16)
        if with_gates:
            gt_ref[...] = lax.dot_general(wgt_ref[...], h_scr[...], _NT, preferred_element_type=F32) + bg_ref[...]

    o_ref[...] = jnp.dot(h_scr[...], w_ref[...], preferred_element_type=F32).astype(o_ref.dtype)


def _in_projection(xs, g, mods, w, wgt, bg, *, sel):
    t, d = xs.shape
    n = w.shape[1]
    tn = _col_tile(n)
    tm = sel["tm"]
    mp = mods.shape[0]
    with_gates = wgt is not None
    in_specs = [
        pl.BlockSpec((tm, d), lambda i, j: (i, 0)),
        pl.BlockSpec((1, d), lambda i, j: (0, 0)),
        pl.BlockSpec((mp, d), lambda i, j: (0, 0)),
        pl.BlockSpec((mp, d), lambda i, j: (0, 1)),
        pl.BlockSpec((d, tn), lambda i, j: (0, j)),
    ]
    out_specs = [pl.BlockSpec((tm, tn), lambda i, j: (i, j))]
    out_shape = [jax.ShapeDtypeStruct((t, n), BF16)]
    args = [xs, g.reshape(1, d), mods, mods, w]
    if with_gates:
        ng = wgt.shape[0]
        in_specs += [pl.BlockSpec((ng, d), lambda i, j: (0, 0)), pl.BlockSpec((ng, 1), lambda i, j: (0, 0))]
        out_specs.append(pl.BlockSpec((ng, tm), lambda i, j: (0, i)))
        out_shape.append(jax.ShapeDtypeStruct((ng, t), F32))
        args += [wgt, bg.reshape(ng, 1)]
    res = pl.pallas_call(
        functools.partial(_inproj_kernel, with_gates=with_gates, sel=sel),
        name="in_projection",
        grid=(t // tm, n // tn),
        in_specs=in_specs,
        out_specs=out_specs,
        out_shape=out_shape,
        scratch_shapes=[pltpu.VMEM((tm, d), BF16)],
        compiler_params=_cparams(("arbitrary", "arbitrary")),
    )(*args)
    return res if with_gates else res[0]


def _chunk_order(nc, ncc, reverse):
    ctx = list(range(ncc))
    lat = list(range(ncc, nc))
    return (ctx[::-1] + lat[::-1]) if reverse else (ctx + lat)


def _mlstm_kernel(q_ref, k_ref, v_ref, o_ref, gates_ref, gh_ref, out_ref, vaug, yacc, *, nc, ncc):
    L = SCAN_CHUNK
    d = HEAD_DIM
    scale = d ** -0.5
    p_rows = nc * L
    vaug[:, 0:d] = v_ref[...]
    lane = lax.broadcasted_iota(jnp.int32, (p_rows, d), 1)
    vaug[:, d:2 * d] = jnp.where(lane == 0, 1.0, 0.0).astype(BF16)
    gates = gates_ref[...]
    ii = lax.broadcasted_iota(jnp.int32, (L, L), 0)
    jj = lax.broadcasted_iota(jnp.int32, (L, L), 1)
    eye = (ii == jj).astype(F32)
    for dirn in (0, 1):
        li = gates[2 * dirn]
        lf = _log_sigmoid(gates[2 * dirn + 1])
        mask = (jj <= ii) if dirn == 0 else (jj >= ii)
        mask_f = mask.astype(F32)
        b_row = lax.dot_general(lf, mask_f, _NT, precision=HIGHEST, preferred_element_type=F32)
        b_col = lax.dot_general(mask_f, lf, _NT, precision=HIGHEST, preferred_element_type=F32)
        li_col = lax.dot_general(eye, li, _NT, precision=HIGHEST, preferred_element_type=F32)
        a_row = li - b_row
        a_col = li_col - b_col
        c_state = jnp.zeros((d, 2 * d), F32)
        m = jnp.zeros((1, 1), F32)
        for c in _chunk_order(nc, ncc, dirn == 1):
            sl = slice(c * L, (c + 1) * L)
            qc = q_ref[sl, :]
            kc = k_ref[sl, :]
            va = vaug[sl, :]
            bcol = b_col[:, c:c + 1]
            arow = a_row[c:c + 1, :]
            acol = a_col[:, c:c + 1]
            b_end = b_row[c:c + 1, L - 1:L] if dirn == 0 else b_row[c:c + 1, 0:1]
            dm = jnp.where(mask, bcol + arow, -jnp.inf)
            m_row = jnp.maximum(bcol + m, jnp.max(dm, axis=-1, keepdims=True))
            w = jnp.exp(dm - m_row) * scale
            w_inter = jnp.exp(bcol + m - m_row)
            qk = lax.dot_general(qc, kc, _NT, preferred_element_type=F32)
            s = (qk * w).astype(BF16)
            r = (jnp.dot(s, va, preferred_element_type=F32)
                 + w_inter * jnp.dot(qc, c_state.astype(BF16), preferred_element_type=F32))
            h = r[:, 0:d] / jnp.maximum(jnp.abs(r[:, d:d + 1]), jnp.exp(-m_row))
            if dirn == 0:
                yacc[sl, :] = h
            else:
                yacc[sl, :] = yacc[sl, :] + h
            m_new = jnp.maximum(b_end + m, jnp.max(b_end + arow, axis=-1, keepdims=True))
            w_end = jnp.exp(b_end + acol - m_new) * scale
            w_prev = jnp.exp(b_end + m - m_new)
            kw = (kc.astype(F32) * w_end).astype(BF16)
            c_state = w_prev * c_state + lax.dot_general(kw, va, _TN, preferred_element_type=F32)
            m = m_new
    y = yacc[...]
    yn = y * lax.rsqrt(jnp.mean(y * y, axis=-1, keepdims=True) + EPS) * gh_ref[...]
    out_ref[...] = (_sigmoid(o_ref[...].astype(F32)) * yn).astype(out_ref.dtype)


def _mlstm(proj, gates, g_head, *, nb, heads, p_rows, lc, width):
    nc = p_rows // SCAN_CHUNK
    ncc = lc // SCAN_CHUNK
    ncp = gates.shape[3]
    d = HEAD_DIM
    t = proj.shape[0]

    def col(base):
        return pl.BlockSpec((p_rows, d), lambda b, h: (b, base * heads + h))

    return pl.pallas_call(
        functools.partial(_mlstm_kernel, nc=nc, ncc=ncc),
        name="mlstm",
        grid=(nb, heads),
        in_specs=[col(0), col(1), col(2), col(3),
                  pl.BlockSpec((4, None, None, ncp, SCAN_CHUNK), lambda b, h: (0, h, b, 0, 0)),
                  pl.BlockSpec((None, 1, d), lambda b, h: (h, 0, 0))],
        out_specs=pl.BlockSpec((p_rows, d), lambda b, h: (b, h)),
        out_shape=jax.ShapeDtypeStruct((t, width), BF16),
        scratch_shapes=[pltpu.VMEM((p_rows, 2 * d), BF16), pltpu.VMEM((p_rows, d), F32)],
        compiler_params=_cparams(("arbitrary", "arbitrary")),
    )(proj, proj, proj, proj, gates, g_head.reshape(heads, 1, d))


def _softmax_pv(parts):
    m = parts[0][0].max(axis=-1, keepdims=True)
    for s, _ in parts[1:]:
        m = jnp.maximum(m, s.max(axis=-1, keepdims=True))
    num = 0.0
    den = 0.0
    for s, v in parts:
        p = jnp.exp(s - m)
        den = den + jnp.sum(p, axis=-1, keepdims=True)
        num = num + jnp.dot(p.astype(BF16), v, preferred_element_type=F32)
    return num / den


def _natten_kernel(q_ref, k_ref, v_ref, bias_ref, out_ref, *, lc, rows):
    scale = HEAD_DIM ** -0.5
    kc = k_ref[0:lc, :]
    vc = v_ref[0:lc, :]
    s_cc = lax.dot_general(q_ref[0:lc, :], kc, _NT, preferred_element_type=F32) * scale
    out_ref[0:lc, :] = _softmax_pv([(s_cc, vc)]).astype(out_ref.dtype)
    n_blocks = rows // NA_Q_ROWS
    for rb in range(n_blocks):
        r = rb * NA_Q_ROWS
        ks = min(max(r - NA_WIN_R // 2, 0), rows - NA_K_ROWS)
        pattern = 0 if rb == 0 else (2 if rb == n_blocks - 1 else 1)
        q_sl = slice(lc + r * GRID_W, lc + (r + NA_Q_ROWS) * GRID_W)
        k_sl = slice(lc + ks * GRID_W, lc + (ks + NA_K_ROWS) * GRID_W)
        qr = q_ref[q_sl, :]
        kb = k_ref[k_sl, :]
        vb = v_ref[k_sl, :]
        s_lat = lax.dot_general(qr, kb, _NT, preferred_element_type=F32) * scale + bias_ref[pattern]
        s_ctx = lax.dot_general(qr, kc, _NT, preferred_element_type=F32) * scale
        out_ref[q_sl, :] = _softmax_pv([(s_lat, vb), (s_ctx, vc)]).astype(out_ref.dtype)


def _natten_bias(rpb, rows):
    assert rows % NA_Q_ROWS == 0 and rows >= NA_K_ROWS
    col = np.arange(GRID_W)
    c0 = np.clip(col - NA_WIN_C // 2, 0, GRID_W - NA_WIN_C)
    col_ok = (col[None, :] >= c0[:, None]) & (col[None, :] < c0[:, None] + NA_WIN_C)
    dc_idx = np.clip(col[None, :] - col[:, None], -(NA_WIN_C - 1), NA_WIN_C - 1) + NA_WIN_C - 1
    bias_c = jnp.where(col_ok[None, None], rpb[:, :, dc_idx], NEG_BIG)
    masked = jnp.full(bias_c[:, 0].shape, NEG_BIG, F32)
    half = NA_WIN_R // 2
    patterns = [(0, lambda a: 0), (-half, lambda a: a), (-NA_WIN_R, lambda a: half)]
    tables = []
    for delta, band0 in patterns:
        q_rows = []
        for a in range(NA_Q_ROWS):
            blocks = []
            for j in range(NA_K_ROWS):
                in_band = band0(a) <= j < band0(a) + NA_WIN_R
                blocks.append(bias_c[:, j + delta - a + NA_WIN_R - 1] if in_band else masked)
            q_rows.append(jnp.concatenate(blocks, axis=-1))
        tables.append(jnp.concatenate(q_rows, axis=-2))
    return jnp.stack(tables, axis=1)


def _natten(proj, bias, *, nb, heads, p_rows, lc, width, base):
    d = HEAD_DIM
    t = proj.shape[0]
    rows = (p_rows - lc) // GRID_W

    def col(k):
        return pl.BlockSpec((p_rows, d), lambda b, h: (b, (base + k) * heads + h))

    return pl.pallas_call(
        functools.partial(_natten_kernel, lc=lc, rows=rows),
        name="natten",
        grid=(nb, heads),
        in_specs=[col(0), col(1), col(2),
                  pl.BlockSpec((None,) + bias.shape[1:], lambda b, h: (h, 0, 0, 0))],
        out_specs=pl.BlockSpec((p_rows, d), lambda b, h: (b, h)),
        out_shape=jax.ShapeDtypeStruct((t, width), BF16),
        compiler_params=_cparams(("arbitrary", "arbitrary")),
    )(proj, proj, proj, bias)


def _head_norm(x, g):
    return x * lax.rsqrt(jnp.mean(x * x, axis=-1, keepdims=True) + EPS) * g


def _gqa_kernel(q_ref, k_ref, v_ref, gq_ref, gk_ref, cos_ref, sin_ref, out_ref, kn_scr, *, lc, tq):
    d = HEAD_DIM
    scale = d ** -0.5
    qt = pl.program_id(2)
    ncc = lc // tq

    @pl.when(qt == 0)
    def _():
        kn = _head_norm(k_ref[...].astype(F32), gk_ref[...])
        kn_scr[0:lc, :] = kn[0:lc].astype(BF16)
        kn_scr[lc:, :] = _rope(kn[lc:], cos_ref[...], sin_ref[...]).astype(BF16)

    def attend(rotate, n_keys):
        for g in range(Q_PER_KV):
            qn = _head_norm(q_ref[:, g * d:(g + 1) * d].astype(F32), gq_ref[...])
            if rotate:
                off = pl.multiple_of((qt - ncc) * tq, tq)
                qn = _rope(qn, cos_ref[pl.ds(off, tq), :], sin_ref[pl.ds(off, tq), :])
            qb = (qn * scale).astype(BF16)
            s = lax.dot_general(qb, kn_scr[0:n_keys, :], _NT, preferred_element_type=F32)
            out_ref[:, g * d:(g + 1) * d] = _softmax_pv([(s, v_ref[0:n_keys, :])]).astype(out_ref.dtype)

    @pl.when(qt < ncc)
    def _():
        attend(False, lc)

    @pl.when(qt >= ncc)
    def _():
        attend(True, kn_scr.shape[0])


def _gqa(proj, g_q, g_k, cosf, sinf, *, nb, heads, p_rows, lc, width):
    d = HEAD_DIM
    kvh = heads // Q_PER_KV
    t = proj.shape[0]
    tq = 256
    nq = p_rows // tq
    s_rows = cosf.shape[0]
    gw = Q_PER_KV * d
    return pl.pallas_call(
        functools.partial(_gqa_kernel, lc=lc, tq=tq),
        name="gqa",
        grid=(nb, kvh, nq),
        in_specs=[pl.BlockSpec((tq, gw), lambda b, kh, i: (b * nq + i, kh)),
                  pl.BlockSpec((p_rows, d), lambda b, kh, i: (b, heads + kh)),
                  pl.BlockSpec((p_rows, d), lambda b, kh, i: (b, heads + kvh + kh)),
                  pl.BlockSpec((1, d), lambda b, kh, i: (0, 0)),
                  pl.BlockSpec((1, d), lambda b, kh, i: (0, 0)),
                  pl.BlockSpec((s_rows, d), lambda b, kh, i: (0, 0)),
                  pl.BlockSpec((s_rows, d), lambda b, kh, i: (0, 0))],
        out_specs=pl.BlockSpec((tq, gw), lambda b, kh, i: (b * nq + i, kh)),
        out_shape=jax.ShapeDtypeStruct((t, width), BF16),
        scratch_shapes=[pltpu.VMEM((p_rows, d), BF16)],
        compiler_params=_cparams(("arbitrary", "arbitrary", "arbitrary")),
    )(proj, proj, proj, g_q.reshape(1, d), g_k.reshape(1, d), cosf, sinf)


def _retention_kernel(q_ref, k_ref, v_ref, g_ref, lg_ref, gh_ref, cos_ref, sin_ref, out_ref,
                      qs, ks, yacc, *, nc, ncc, lc):
    L = SCAN_CHUNK
    d = HEAD_DIM
    scale = d ** -0.5
    cosf = cos_ref[...]
    sinf = sin_ref[...]
    qs[0:lc, :] = q_ref[0:lc, :]
    ks[0:lc, :] = k_ref[0:lc, :]
    qs[lc:, :] = _rope(q_ref[lc:, :].astype(F32), cosf, sinf).astype(BF16)
    ks[lc:, :] = _rope(k_ref[lc:, :].astype(F32), cosf, sinf).astype(BF16)
    ii = lax.broadcasted_iota(jnp.int32, (L, L), 0)
    jj = lax.broadcasted_iota(jnp.int32, (L, L), 1)
    pos = lax.broadcasted_iota(jnp.int32, (L, 1), 0).astype(F32)
    for dirn in (0, 1):
        lg = _log_sigmoid(lg_ref[dirn:dirn + 1, 0:1])
        rel = (ii - jj) if dirn == 0 else (jj - ii)
        decay = jnp.where(rel >= 0, jnp.exp(jnp.maximum(rel, 0).astype(F32) * lg), 0.0) * scale
        if dirn == 0:
            dq = jnp.exp((pos + 1.0) * lg)
            dk = jnp.exp((L - 1.0 - pos) * lg) * scale
        else:
            dq = jnp.exp((L - pos) * lg)
            dk = jnp.exp(pos * lg) * scale
        dchunk = jnp.exp(L * lg)
        state = jnp.zeros((d, d), F32)
        for c in _chunk_order(nc, ncc, dirn == 1):
            sl = slice(c * L, (c + 1) * L)
            qc = qs[sl, :]
            kc = ks[sl, :]
            vc = v_ref[sl, :]
            s = (lax.dot_general(qc, kc, _NT, preferred_element_type=F32) * decay).astype(BF16)
            o = (jnp.dot(s, vc, preferred_element_type=F32)
                 + dq * jnp.dot(qc, state.astype(BF16), preferred_element_type=F32))
            if dirn == 0:
                yacc[sl, :] = o
            else:
                yacc[sl, :] = yacc[sl, :] + o
            kw = (kc.astype(F32) * dk).astype(BF16)
            state = dchunk * state + lax.dot_general(kw, vc, _TN, preferred_element_type=F32)
    y = yacc[...]
    yn = y * lax.rsqrt(jnp.mean(y * y, axis=-1, keepdims=True) + EPS) * gh_ref[...]
    gate = g_ref[...].astype(F32)
    out_ref[...] = (gate * _sigmoid(gate) * yn).astype(out_ref.dtype)


def _retention(proj, decay_logit, g_head, cosf, sinf, *, nb, heads, p_rows, lc, width, base):
    d = HEAD_DIM
    nc = p_rows // SCAN_CHUNK
    ncc = lc // SCAN_CHUNK
    t = proj.shape[0]
    s_rows = cosf.shape[0]
    lg = jnp.broadcast_to(decay_logit.T[:, :, None], (heads, 2, d)).astype(F32)
    lg = jnp.concatenate([lg, jnp.zeros((heads, SUBLANES - 2, d), F32)], axis=1)

    def col(k):
        return pl.BlockSpec((p_rows, d), lambda b, h: (b, base + k * heads + h))

    return pl.pallas_call(
        functools.partial(_retention_kernel, nc=nc, ncc=ncc, lc=lc),
        name="retention",
        grid=(nb, heads),
        in_specs=[col(0), col(1), col(2), col(3),
                  pl.BlockSpec((None, SUBLANES, d), lambda b, h: (h, 0, 0)),
                  pl.BlockSpec((None, 1, d), lambda b, h: (h, 0, 0)),
                  pl.BlockSpec((s_rows, d), lambda b, h: (0, 0)),
                  pl.BlockSpec((s_rows, d), lambda b, h: (0, 0))],
        out_specs=pl.BlockSpec((p_rows, d), lambda b, h: (b, h)),
        out_shape=jax.ShapeDtypeStruct((t, width), BF16),
        scratch_shapes=[pltpu.VMEM((p_rows, d), BF16), pltpu.VMEM((p_rows, d), BF16),
                        pltpu.VMEM((p_rows, d), F32)],
        compiler_params=_cparams(("arbitrary", "arbitrary")),
    )(proj, proj, proj, proj, lg, g_head.reshape(heads, 1, d), cosf, sinf)


def _outproj_kernel(a_ref, b_ref, wa_ref, wb_ref, x_ref, gt_ref, o_ref, *, sel):
    i = pl.program_id(0)
    y = (jnp.dot(a_ref[...], wa_ref[...], preferred_element_type=F32)
         + jnp.dot(b_ref[...], wb_ref[...], preferred_element_type=F32))
    o_ref[...] = x_ref[...] + _select_rows(gt_ref, i, **sel) * y


def _out_projection(mix_a, mix_b, w_a, w_b, xs, mods, *, sel, gate_chunk):
    t, d = xs.shape
    gw = mix_a.shape[1]
    tm = sel["tm"]
    tn = min(1024, d)
    mp = mods.shape[0]
    npd = d // tn
    return pl.pallas_call(
        functools.partial(_outproj_kernel, sel=sel),
        name="out_projection",
        grid=(t // tm, npd),
        in_specs=[pl.BlockSpec((tm, gw), lambda i, j: (i, 0)),
                  pl.BlockSpec((tm, gw), lambda i, j: (i, 0)),
                  pl.BlockSpec((gw, tn), lambda i, j: (0, j)),
                  pl.BlockSpec((gw, tn), lambda i, j: (0, j)),
                  pl.BlockSpec((tm, tn), lambda i, j: (i, j)),
                  pl.BlockSpec((mp, tn), lambda i, j: (0, gate_chunk * npd + j))],
        out_specs=pl.BlockSpec((tm, tn), lambda i, j: (i, j)),
        out_shape=jax.ShapeDtypeStruct((t, d), F32),
        compiler_params=_cparams(("arbitrary", "arbitrary")),
    )(mix_a, mix_b, w_a, w_b, xs, mods)


def _first_argmax(vals):
    best = vals[0]
    idx = jnp.zeros_like(best)
    for j in range(1, len(vals)):
        upd = vals[j] > best
        idx = jnp.where(upd, float(j), idx)
        best = jnp.where(upd, vals[j], best)
    return idx, best


def _pick(idx, vals):
    out = vals[-1]
    for j in range(len(vals) - 2, -1, -1):
        out = jnp.where(idx == float(j), vals[j], out)
    return out


def _ffn_pre_kernel(x_ref, g_ref, sh_ref, sc_ref, wrt_ref, br_ref, h_ref, r_ref, *, sel):
    i = pl.program_id(0)
    h = _norm_mod(x_ref[...], g_ref[...], sh_ref, sc_ref, i, **sel)
    h_ref[...] = h
    logits = lax.dot_general(wrt_ref[...], h, _NT, precision=HIGHEST, preferred_element_type=F32)
    aff = _sigmoid(logits)
    sel_s = aff + br_ref[...]
    a = [aff[e:e + 1, :] for e in range(N_EXPERTS)]
    s = [sel_s[e:e + 1, :] for e in range(N_EXPERTS)]
    n = EXPERTS_PER_GROUP
    scores = []
    for grp in range(N_GROUPS):
        v = s[grp * n:(grp + 1) * n]
        best = v[0] + v[1]
        for p in range(n):
            for q in range(p + 1, n):
                if (p, q) != (0, 1):
                    best = jnp.maximum(best, v[p] + v[q])
        scores.append(best)
    gi, _ = _first_argmax(scores)
    cv = [_pick(gi, [s[grp * n + j] for grp in range(N_GROUPS)]) for j in range(n)]
    av = [_pick(gi, [a[grp * n + j] for grp in range(N_GROUPS)]) for j in range(n)]
    i1, _ = _first_argmax(cv)
    cv2 = [jnp.where(i1 == float(j), -jnp.inf, cv[j]) for j in range(n)]
    i2, _ = _first_argmax(cv2)
    w1 = _pick(i1, av)
    w2 = _pick(i2, av)
    tot = w1 + w2
    r_ref[0:1, :] = gi * float(n) + i1
    r_ref[1:2, :] = gi * float(n) + i2
    r_ref[2:3, :] = w1 / tot
    r_ref[3:4, :] = w2 / tot
    r_ref[4:8, :] = jnp.zeros((4, r_ref.shape[1]), F32)


def _ffn_pre(xs, g, mods, w_router_t, b_router, *, sel):
    t, d = xs.shape
    tm = sel["tm"]
    mp = mods.shape[0]
    ne = w_router_t.shape[0]
    return pl.pallas_call(
        functools.partial(_ffn_pre_kernel, sel=sel),
        name="ffn_pre_router",
        grid=(t // tm,),
        in_specs=[pl.BlockSpec((tm, d), lambda i: (i, 0)),
                  pl.BlockSpec((1, d), lambda i: (0, 0)),
                  pl.BlockSpec((mp, d), lambda i: (0, 3)),
                  pl.BlockSpec((mp, d), lambda i: (0, 4)),
                  pl.BlockSpec((ne, d), lambda i: (0, 0)),
                  pl.BlockSpec((ne, 1), lambda i: (0, 0))],
        out_specs=[pl.BlockSpec((tm, d), lambda i: (i, 0)),
                   pl.BlockSpec((SUBLANES, tm), lambda i: (0, i))],
        out_shape=[jax.ShapeDtypeStruct((t, d), F32), jax.ShapeDtypeStruct((SUBLANES, t), F32)],
        compiler_params=_cparams(("arbitrary",)),
    )(xs, g.reshape(1, d), mods, mods, w_router_t, b_router.reshape(ne, 1))


def _dispatch_metadata(route, tm):
    t = route.shape[1]
    na = 2 * t
    assert na % tm == 0
    i32 = jnp.int32
    ne = N_EXPERTS
    n_pad = ne * tm
    e = route[0:2].astype(i32).reshape(na)
    expert_ids = jnp.arange(ne, dtype=i32)
    counts = jnp.sum((e[:, None] == expert_ids[None, :]).astype(i32), axis=0)
    pad_end = jnp.cumsum((-counts) % tm)
    pad_id = jnp.arange(n_pad, dtype=i32)
    pad_key = jnp.sum((pad_id[:, None] >= pad_end[None, :]).astype(i32), axis=1)
    a_id = jnp.arange(na, dtype=i32)
    keys = jnp.concatenate([e, pad_key])
    dst = jnp.concatenate([a_id, na + pad_id])
    src = jnp.concatenate([jnp.where(a_id >= t, a_id - t, a_id), jnp.zeros((n_pad,), i32)])
    keys_s, dst_s, src_s = lax.sort((keys, dst, src), num_keys=1, is_stable=True)
    te = keys_s[0::tm]
    last_e = jnp.max(jnp.where(counts > 0, expert_ids, 0))
    te = jnp.where(te < ne, te, last_e)
    n_slots = na + n_pad
    lead_dst = n_slots + jnp.arange(tm, dtype=i32)
    zeros = jnp.zeros((tm,), i32)
    te_x = jnp.concatenate([te[:1], te, te[-1:]]).astype(i32)
    src_x = jnp.concatenate([zeros, src_s, zeros]).astype(i32)
    dst_x = jnp.concatenate([lead_dst, dst_s, zeros]).astype(i32)
    return te_x, src_x, dst_x


def _row_copy(src_hbm, row, dst, dst_row, sem):
    return pltpu.make_async_copy(src_hbm.at[pl.ds(row, 1), :], dst.at[pl.ds(dst_row, 1), :], sem)


def _moe_kernel(te_ref, src_ref, dst_ref, h_hbm, wg_ref, wu_ref, wd_ref, y_hbm,
                x0, x1, acc0, acc1, gsem, ssem, *, tm, n_tiles):
    del te_ref
    i = pl.program_id(0)
    xbufs = (x0, x1)
    accs = (acc0, acc1)

    def gather_copy(entry, r, buf, sem):
        return _row_copy(h_hbm, src_ref[entry * tm + r], buf, r, sem)

    def scatter_copy(entry, r, buf, sem):
        row = dst_ref[entry * tm + r]
        return pltpu.make_async_copy(buf.at[pl.ds(r, 1), :], y_hbm.at[pl.ds(row, 1), :], sem)

    def for_rows(fn):
        def body(r, carry):
            fn(r)
            return carry
        lax.fori_loop(0, tm, body, 0, unroll=8)

    @pl.when(i == 0)
    def _():
        for_rows(lambda r: gather_copy(1, r, x0, gsem.at[0]).start())
        acc1[...] = jnp.zeros(acc1.shape, acc1.dtype)

    for par in (0, 1):
        cur_x, nxt_x = xbufs[par], xbufs[1 - par]
        cur_acc, prev_acc = accs[par], accs[1 - par]

        @pl.when(jnp.logical_and(jnp.logical_and(i >= 1, i <= n_tiles), (i - 1) % 2 == par))
        def _(cur_x=cur_x, nxt_x=nxt_x, cur_acc=cur_acc, prev_acc=prev_acc, par=par):
            for_rows(lambda r: gather_copy(0, r, cur_x, gsem.at[par]).wait())

            @pl.when(i >= 2)
            def _():
                for_rows(lambda r: scatter_copy(0, r, cur_acc, ssem.at[par]).wait())

            for r in range(tm):
                gather_copy(i + 1, r, nxt_x, gsem.at[1 - par]).start()
            for r in range(tm):
                scatter_copy(i - 1, r, prev_acc, ssem.at[1 - par]).start()
            xb = cur_x[...].astype(BF16)
            g = jnp.dot(xb, wg_ref[...], preferred_element_type=F32)
            u = jnp.dot(xb, wu_ref[...], preferred_element_type=F32)
            act = (g * _sigmoid(g) * u).astype(BF16)
            cur_acc[...] = jnp.dot(act, wd_ref[...], preferred_element_type=F32)

    @pl.when(i == n_tiles + 1)
    def _():
        p_last = (n_tiles - 1) % 2
        p_dummy = n_tiles % 2
        for_rows(lambda r: gather_copy(0, r, xbufs[p_dummy], gsem.at[p_dummy]).wait())
        for_rows(lambda r: scatter_copy(0, r, accs[p_dummy], ssem.at[p_dummy]).wait())
        for_rows(lambda r: scatter_copy(n_tiles, r, accs[p_last], ssem.at[p_last]).start())
        for_rows(lambda r: scatter_copy(0, r, accs[p_last], ssem.at[p_last]).wait())


def _moe_experts(h2, meta, wg, wu, wd):
    te_x, src_x, dst_x = meta
    t, d = h2.shape
    tm = MOE_TILE
    n_tiles = te_x.shape[0] - 2
    ff = wg.shape[2]
    grid_spec = pltpu.PrefetchScalarGridSpec(
        num_scalar_prefetch=3,
        grid=(n_tiles + 2,),
        in_specs=[pl.BlockSpec(memory_space=pl.ANY),
                  pl.BlockSpec((None, d, ff), lambda i, te, *_: (te[i], 0, 0)),
                  pl.BlockSpec((None, d, ff), lambda i, te, *_: (te[i], 0, 0)),
                  pl.BlockSpec((None, ff, d), lambda i, te, *_: (te[i], 0, 0))],
        out_specs=pl.BlockSpec(memory_space=pl.ANY),
        scratch_shapes=[pltpu.VMEM((tm, d), F32), pltpu.VMEM((tm, d), F32),
                        pltpu.VMEM((tm, d), F32), pltpu.VMEM((tm, d), F32),
                        pltpu.SemaphoreType.DMA((2,)), pltpu.SemaphoreType.DMA((2,))],
    )
    return pl.pallas_call(
        functools.partial(_moe_kernel, tm=tm, n_tiles=n_tiles),
        name="moe_experts",
        grid_spec=grid_spec,
        out_shape=jax.ShapeDtypeStruct(((n_tiles + 1) * tm, d), F32),
        compiler_params=_cparams(("arbitrary",)),
    )(te_x, src_x, dst_x, h2, wg, wu, wd)


def _combine_kernel(y0_ref, y1_ref, w0_ref, w1_ref, x_ref, gt_ref, o_ref, *, sel):
    i = pl.program_id(0)
    mix = w0_ref[...] * y0_ref[...] + w1_ref[...] * y1_ref[...]
    o_ref[...] = x_ref[...] + _select_rows(gt_ref, i, **sel) * mix


def _moe_combine(ys, route, xs, mods, *, sel):
    t, d = xs.shape
    tm = sel["tm"]
    nt = t // tm
    mp = mods.shape[0]
    w0 = route[2].reshape(t, 1)
    w1 = route[3].reshape(t, 1)
    return pl.pallas_call(
        functools.partial(_combine_kernel, sel=sel),
        name="moe_combine",
        grid=(nt,),
        in_specs=[pl.BlockSpec((tm, d), lambda i: (i, 0)),
                  pl.BlockSpec((tm, d), lambda i: (nt + i, 0)),
                  pl.BlockSpec((tm, 1), lambda i: (i, 0)),
                  pl.BlockSpec((tm, 1), lambda i: (i, 0)),
                  pl.BlockSpec((tm, d), lambda i: (i, 0)),
                  pl.BlockSpec((mp, d), lambda i: (0, 5))],
        out_specs=pl.BlockSpec((tm, d), lambda i: (i, 0)),
        out_shape=jax.ShapeDtypeStruct((t, d), F32),
        compiler_params=_cparams(("arbitrary",)),
    )(ys, ys, w0, w1, xs, mods)


def _final_norm_kernel(x_ref, g_ref, o_ref):
    x = x_ref[...]
    o_ref[...] = x * lax.rsqrt(jnp.mean(x * x, axis=-1, keepdims=True) + EPS) * g_ref[...]


def _final_norm(xs, g, *, nb, p_rows, lc):
    t, d = xs.shape
    tm = 256
    tpb = p_rows // tm
    ncc = lc // tm
    lat = tpb - ncc
    return pl.pallas_call(
        _final_norm_kernel,
        name="final_norm",
        grid=(nb, lat),
        in_specs=[pl.BlockSpec((tm, d), lambda b, j: (b * tpb + ncc + j, 0)),
                  pl.BlockSpec((1, d), lambda b, j: (0, 0))],
        out_specs=pl.BlockSpec((tm, d), lambda b, j: (b * lat + j, 0)),
        out_shape=jax.ShapeDtypeStruct((nb * lat * tm, d), F32),
        compiler_params=_cparams(("arbitrary", "arbitrary")),
    )(xs, g.reshape(1, d))


def _rope_tables(n_tokens):
    tpos = np.arange(n_tokens)
    row = (tpos // GRID_W).astype(np.float32)
    col = (tpos % GRID_W).astype(np.float32)
    n_axis = HEAD_DIM // 4
    inv = jnp.asarray(ROPE_BASE, F32) ** (-jnp.arange(n_axis, dtype=F32) / n_axis)
    ang = jnp.concatenate([jnp.asarray(row)[:, None] * inv, jnp.asarray(col)[:, None] * inv], axis=-1)
    cos, sin = jnp.cos(ang), jnp.sin(ang)
    return jnp.concatenate([cos, cos], axis=-1), jnp.concatenate([-sin, sin], axis=-1)


def kernel(x, c, ctx, c_ctx, w_mod, b_mod, g_norm_mix, g_norm_ffn, g_norm_out, w_in_ab, b_gate_ab, g_mlstm, rpb_na, w_out_ab, w_in_cd, g_qnorm, g_knorm, ret_decay_logit, g_ret, w_out_cd, w_router, b_router, w_exp_gate, w_exp_up, w_exp_down):
    nb, s_len, d = x.shape
    lc = ctx.shape[1]
    p_rows = lc + s_len
    heads = d // (2 * HEAD_DIM)
    kvh = heads // Q_PER_KV
    gw = heads * HEAD_DIM
    depth = w_mod.shape[0]
    assert lc % SCAN_CHUNK == 0 and s_len % SCAN_CHUNK == 0 and heads % Q_PER_KV == 0

    xs = jnp.concatenate([ctx, x], axis=1).reshape(nb * p_rows, d)
    mp = -(-(nb + 1) // SUBLANES) * SUBLANES
    cc = jnp.concatenate([c, c_ctx[None, :], jnp.zeros((mp - nb - 1, d), F32)], axis=0)
    mods = _modulation(cc, w_mod, b_mod)

    tm = _row_tile(p_rows)
    sel = dict(lc=lc, tm=tm, tpb=p_rows // tm, nb=nb)
    sel_c = dict(lc=lc, tm=COMBINE_TILE, tpb=p_rows // COMBINE_TILE, nb=nb)
    cosf, sinf = _rope_tables(s_len)
    nc = p_rows // SCAN_CHUNK
    ncp = -(-nc // SUBLANES) * SUBLANES
    seq = dict(nb=nb, heads=heads, p_rows=p_rows, lc=lc, width=gw)
    w_router_t = w_router.T.astype(F32)

    for layer in range(depth):
        ml = mods[layer]
        p = layer // 2
        if layer % 2 == 0:
            w = w_in_ab[p]
            ng = 4 * heads
            w_main = jnp.concatenate([w[:, :4 * gw], w[:, 4 * gw + ng:]], axis=1).astype(BF16)
            w_gate_t = w[:, 4 * gw:4 * gw + ng].T.astype(BF16)
            proj, gates_t = _in_projection(xs, g_norm_mix[layer], ml, w_main, w_gate_t, b_gate_ab[p], sel=sel)
            gates = gates_t.reshape(4, heads, nb, nc, SCAN_CHUNK)
            gates = jnp.pad(gates, ((0, 0), (0, 0), (0, 0), (0, ncp - nc), (0, 0)))
            mix_a = _mlstm(proj, gates, g_mlstm[p], **seq)
            mix_b = _natten(proj, _natten_bias(rpb_na[p], s_len // GRID_W), base=4, **seq)
            w_out = w_out_ab[p]
        else:
            proj = _in_projection(xs, g_norm_mix[layer], ml, w_in_cd[p].astype(BF16), None, None, sel=sel)
            mix_a = _gqa(proj, g_qnorm[p], g_knorm[p], cosf, sinf, **seq)
            mix_b = _retention(proj, ret_decay_logit[p], g_ret[p], cosf, sinf, base=heads + 2 * kvh, **seq)
            w_out = w_out_cd[p]
        xs = _out_projection(mix_a, mix_b, w_out[:gw].astype(BF16), w_out[gw:].astype(BF16), xs, ml,
                             sel=sel, gate_chunk=2)
        h2, route = _ffn_pre(xs, g_norm_ffn[layer], ml, w_router_t, b_router, sel=sel)
        ys = _moe_experts(h2, _dispatch_metadata(route, MOE_TILE), w_exp_gate[layer].astype(BF16),
                          w_exp_up[layer].astype(BF16), w_exp_down[layer].astype(BF16))
        xs = _moe_combine(ys, route, xs, ml, sel=sel_c)
    return _final_norm(xs, g_norm_out, nb=nb, p_rows=p_rows, lc=lc).reshape(nb, s_len, d)
```

```python
import functools

import jax
import jax.numpy as jnp
import numpy as np
from jax import lax
from jax.experimental import pallas as pl
from jax.experimental.pallas import tpu as pltpu

F32 = jnp.float32
BF16 = jnp.bfloat16
HIGHEST = lax.Precision.HIGHEST

HEAD_DIM = 128
GRID_W = 64
NA_WIN_R = 8
NA_WIN_C = 16
ROPE_BASE = 10000.0
N_EXPERTS = 16
N_GROUPS = 4
EXPERTS_PER_GROUP = N_EXPERTS // N_GROUPS
N_MOD = 6
EPS = 1e-6
Q_PER_KV = 4
SCAN_CHUNK = 256
LANES = 128
SUBLANES = 8
MOE_TILE = 256
MAX_COL_TILE = 1536
COMBINE_TILE = 256
NA_Q_ROWS = 4
NA_K_ROWS = NA_Q_ROWS + NA_WIN_R
NEG_BIG = -1e30
VMEM_LIMIT = 56 * 1024 * 1024

_NT = (((1,), (1,)), ((), ()))
_TN = (((0,), (0,)), ((), ()))


def _cparams(sem):
    return pltpu.CompilerParams(dimension_semantics=sem, vmem_limit_bytes=VMEM_LIMIT)


def _sigmoid(x):
    return 1.0 / (1.0 + jnp.exp(-x))


def _log_sigmoid(x):
    return jnp.minimum(x, 0.0) - jnp.log(1.0 + jnp.exp(-jnp.abs(x)))


def _row_tile(p_rows):
    best = 256
    for t in (512, 768):
        if p_rows % t == 0:
            best = t
    return best


def _col_tile(n_cols):
    assert n_cols % LANES == 0
    return max(t for t in range(LANES, MAX_COL_TILE + 1, LANES) if n_cols % t == 0)


def _select_rows(ref, i, *, lc, tm, tpb, nb):
    b = i // tpb
    j = i % tpb
    per_sample = ref[pl.ds(b, 1), :]
    ctx = ref[nb:nb + 1, :]
    pos = j * tm + lax.broadcasted_iota(jnp.int32, (tm, 1), 0)
    return jnp.where(pos < lc, ctx, per_sample)


def _norm_mod(x, g, sh_ref, sc_ref, i, **kw):
    ms = jnp.mean(x * x, axis=-1, keepdims=True)
    y = x * lax.rsqrt(ms + EPS) * g
    return y * (1.0 + _select_rows(sc_ref, i, **kw)) + _select_rows(sh_ref, i, **kw)


def _rope(x, cosf, sinf):
    return x * cosf + pltpu.roll(x, HEAD_DIM // 2, 1) * sinf


def _mod_kernel(cc_ref, w_ref, b_ref, o_ref):
    a = cc_ref[...]
    a = a * _sigmoid(a)
    o_ref[0] = jnp.dot(a.astype(BF16), w_ref[0].astype(BF16), preferred_element_type=F32) + b_ref[0]


def _modulation(cc, w_mod, b_mod):
    depth, d, n = w_mod.shape
    mp = cc.shape[0]
    tn = 1024
    return pl.pallas_call(
        _mod_kernel,
        name="modulation",
        grid=(depth, n // tn),
        in_specs=[
            pl.BlockSpec((mp, d), lambda l, j: (0, 0)),
            pl.BlockSpec((1, d, tn), lambda l, j: (l, 0, j)),
            pl.BlockSpec((1, 1, tn), lambda l, j: (l, 0, j)),
        ],
        out_specs=pl.BlockSpec((1, mp, tn), lambda l, j: (l, 0, j)),
        out_shape=jax.ShapeDtypeStruct((depth, mp, n), F32),
        compiler_params=_cparams(("arbitrary", "arbitrary")),
    )(cc, w_mod, b_mod.reshape(depth, 1, n))


def _inproj_kernel(*refs, with_gates, sel):
    if with_gates:
        x_ref, g_ref, sh_ref, sc_ref, w_ref, wgt_ref, bg_ref, o_ref, gt_ref, h_scr = refs
    else:
        x_ref, g_ref, sh_ref, sc_ref, w_ref, o_ref, h_scr = refs
    i = pl.program_id(0)

    @pl.when(pl.program_id(1) == 0)
    def _():
        h = _norm_mod(x_ref[...], g_ref[...], sh_ref, sc_ref, i, **sel)
        h_scr[...] = h.astype(BF16)
        if with_gates:
            gt_ref[...] = lax.dot_general(wgt_ref[...], h_scr[...], _NT, preferred_element_type=F32) + bg_ref[...]

    o_ref[...] = jnp.dot(h_scr[...], w_ref[...], preferred_element_type=F32).astype(o_ref.dtype)


def _in_projection(xs, g, mods, w, wgt, bg, *, sel):
    t, d = xs.shape
    n = w.shape[1]
    tn = _col_tile(n)
    tm = sel["tm"]
    mp = mods.shape[0]
    with_gates = wgt is not None
    in_specs = [
        pl.BlockSpec((tm, d), lambda i, j: (i, 0)),
        pl.BlockSpec((1, d), lambda i, j: (0, 0)),
        pl.BlockSpec((mp, d), lambda i, j: (0, 0)),
        pl.BlockSpec((mp, d), lambda i, j: (0, 1)),
        pl.BlockSpec((d, tn), lambda i, j: (0, j)),
    ]
    out_specs = [pl.BlockSpec((tm, tn), lambda i, j: (i, j))]
    out_shape = [jax.ShapeDtypeStruct((t, n), BF16)]
    args = [xs, g.reshape(1, d), mods, mods, w]
    if with_gates:
        ng = wgt.shape[0]
        in_specs += [pl.BlockSpec((ng, d), lambda i, j: (0, 0)), pl.BlockSpec((ng, 1), lambda i, j: (0, 0))]
        out_specs.append(pl.BlockSpec((ng, tm), lambda i, j: (0, i)))
        out_shape.append(jax.ShapeDtypeStruct((ng, t), F32))
        args += [wgt, bg.reshape(ng, 1)]
    res = pl.pallas_call(
        functools.partial(_inproj_kernel, with_gates=with_gates, sel=sel),
        name="in_projection",
        grid=(t // tm, n // tn),
        in_specs=in_specs,
        out_specs=out_specs,
        out_shape=out_shape,
        scratch_shapes=[pltpu.VMEM((tm, d), BF16)],
        compiler_params=_cparams(("arbitrary", "arbitrary")),
    )(*args)
    return res if with_gates else res[0]


def _chunk_order(nc, ncc, reverse):
    ctx = list(range(ncc))
    lat = list(range(ncc, nc))
    return (ctx[::-1] + lat[::-1]) if reverse else (ctx + lat)


def _mlstm_kernel(q_ref, k_ref, v_ref, o_ref, gates_ref, gh_ref, out_ref, vaug, yacc, *, nc, ncc):
    L = SCAN_CHUNK
    d = HEAD_DIM
    scale = d ** -0.5
    p_rows = nc * L
    vaug[:, 0:d] = v_ref[...]
    lane = lax.broadcasted_iota(jnp.int32, (p_rows, d), 1)
    vaug[:, d:2 * d] = jnp.where(lane == 0, 1.0, 0.0).astype(BF16)
    gates = gates_ref[...]
    ii = lax.broadcasted_iota(jnp.int32, (L, L), 0)
    jj = lax.broadcasted_iota(jnp.int32, (L, L), 1)
    eye = (ii == jj).astype(F32)
    for dirn in (0, 1):
        li = gates[2 * dirn]
        lf = _log_sigmoid(gates[2 * dirn + 1])
        mask = (jj <= ii) if dirn == 0 else (jj >= ii)
        mask_f = mask.astype(F32)
        b_row = lax.dot_general(lf, mask_f, _NT, precision=HIGHEST, preferred_element_type=F32)
        b_col = lax.dot_general(mask_f, lf, _NT, precision=HIGHEST, preferred_element_type=F32)
        li_col = lax.dot_general(eye, li, _NT, precision=HIGHEST, preferred_element_type=F32)
        a_row = li - b_row
        a_col = li_col - b_col
        c_state = jnp.zeros((d, 2 * d), F32)
        m = jnp.zeros((1, 1), F32)
        for c in _chunk_order(nc, ncc, dirn == 1):
            sl = slice(c * L, (c + 1) * L)
            qc = q_ref[sl, :]
            kc = k_ref[sl, :]
            va = vaug[sl, :]
            bcol = b_col[:, c:c + 1]
            arow = a_row[c:c + 1, :]
            acol = a_col[:, c:c + 1]
            b_end = b_row[c:c + 1, L - 1:L] if dirn == 0 else b_row[c:c + 1, 0:1]
            dm = jnp.where(mask, bcol + arow, -jnp.inf)
            m_row = jnp.maximum(bcol + m, jnp.max(dm, axis=-1, keepdims=True))
            w = jnp.exp(dm - m_row) * scale
            w_inter = jnp.exp(bcol + m - m_row)
            qk = lax.dot_general(qc, kc, _NT, preferred_element_type=F32)
            s = (qk * w).astype(BF16)
            r = (jnp.dot(s, va, preferred_element_type=F32)
                 + w_inter * jnp.dot(qc, c_state.astype(BF16), preferred_element_type=F32))
            h = r[:, 0:d] / jnp.maximum(jnp.abs(r[:, d:d + 1]), jnp.exp(-m_row))
            if dirn == 0:
                yacc[sl, :] = h
            else:
                yacc[sl, :] = yacc[sl, :] + h
            m_new = jnp.maximum(b_end + m, jnp.max(b_end + arow, axis=-1, keepdims=True))
            w_end = jnp.exp(b_end + acol - m_new) * scale
            w_prev = jnp.exp(b_end + m - m_new)
            kw = (kc.astype(F32) * w_end).astype(BF16)
            c_state = w_prev * c_state + lax.dot_general(kw, va, _TN, preferred_element_type=F32)
            m = m_new
    y = yacc[...]
    yn = y * lax.rsqrt(jnp.mean(y * y, axis=-1, keepdims=True) + EPS) * gh_ref[...]
    out_ref[...] = (_sigmoid(o_ref[...].astype(F32)) * yn).astype(out_ref.dtype)


def _mlstm(proj, gates, g_head, *, nb, heads, p_rows, lc, width):
    nc = p_rows // SCAN_CHUNK
    ncc = lc // SCAN_CHUNK
    ncp = gates.shape[3]
    d = HEAD_DIM
    t = proj.shape[0]

    def col(base):
        return pl.BlockSpec((p_rows, d), lambda b, h: (b, base * heads + h))

    return pl.pallas_call(
        functools.partial(_mlstm_kernel, nc=nc, ncc=ncc),
        name="mlstm",
        grid=(nb, heads),
        in_specs=[col(0), col(1), col(2), col(3),
                  pl.BlockSpec((4, None, None, ncp, SCAN_CHUNK), lambda b, h: (0, h, b, 0, 0)),
                  pl.BlockSpec((None, 1, d), lambda b, h: (h, 0, 0))],
        out_specs=pl.BlockSpec((p_rows, d), lambda b, h: (b, h)),
        out_shape=jax.ShapeDtypeStruct((t, width), BF16),
        scratch_shapes=[pltpu.VMEM((p_rows, 2 * d), BF16), pltpu.VMEM((p_rows, d), F32)],
        compiler_params=_cparams(("arbitrary", "arbitrary")),
    )(proj, proj, proj, proj, gates, g_head.reshape(heads, 1, d))


def _softmax_pv(parts):
    m = parts[0][0].max(axis=-1, keepdims=True)
    for s, _ in parts[1:]:
        m = jnp.maximum(m, s.max(axis=-1, keepdims=True))
    num = 0.0
    den = 0.0
    for s, v in parts:
        p = jnp.exp(s - m)
        den = den + jnp.sum(p, axis=-1, keepdims=True)
        num = num + jnp.dot(p.astype(BF16), v, preferred_element_type=F32)
    return num / den


def _natten_kernel(q_ref, k_ref, v_ref, bias_ref, out_ref, *, lc, rows):
    scale = HEAD_DIM ** -0.5
    kc = k_ref[0:lc, :]
    vc = v_ref[0:lc, :]
    s_cc = lax.dot_general(q_ref[0:lc, :], kc, _NT, preferred_element_type=F32) * scale
    out_ref[0:lc, :] = _softmax_pv([(s_cc, vc)]).astype(out_ref.dtype)
    n_blocks = rows // NA_Q_ROWS
    for rb in range(n_blocks):
        r = rb * NA_Q_ROWS
        ks = min(max(r - NA_WIN_R // 2, 0), rows - NA_K_ROWS)
        pattern = 0 if rb == 0 else (2 if rb == n_blocks - 1 else 1)
        q_sl = slice(lc + r * GRID_W, lc + (r + NA_Q_ROWS) * GRID_W)
        k_sl = slice(lc + ks * GRID_W, lc + (ks + NA_K_ROWS) * GRID_W)
        qr = q_ref[q_sl, :]
        kb = k_ref[k_sl, :]
        vb = v_ref[k_sl, :]
        s_lat = lax.dot_general(qr, kb, _NT, preferred_element_type=F32) * scale + bias_ref[pattern]
        s_ctx = lax.dot_general(qr, kc, _NT, preferred_element_type=F32) * scale
        out_ref[q_sl, :] = _softmax_pv([(s_lat, vb), (s_ctx, vc)]).astype(out_ref.dtype)


def _natten_bias(rpb, rows):
    assert rows % NA_Q_ROWS == 0 and rows >= NA_K_ROWS
    col = np.arange(GRID_W)
    c0 = np.clip(col - NA_WIN_C // 2, 0, GRID_W - NA_WIN_C)
    col_ok = (col[None, :] >= c0[:, None]) & (col[None, :] < c0[:, None] + NA_WIN_C)
    dc_idx = np.clip(col[None, :] - col[:, None], -(NA_WIN_C - 1), NA_WIN_C - 1) + NA_WIN_C - 1
    bias_c = jnp.where(col_ok[None, None], rpb[:, :, dc_idx], NEG_BIG)
    masked = jnp.full(bias_c[:, 0].shape, NEG_BIG, F32)
    half = NA_WIN_R // 2
    patterns = [(0, lambda a: 0), (-half, lambda a: a), (-NA_WIN_R, lambda a: half)]
    tables = []
    for delta, band0 in patterns:
        q_rows = []
        for a in range(NA_Q_ROWS):
            blocks = []
            for j in range(NA_K_ROWS):
                in_band = band0(a) <= j < band0(a) + NA_WIN_R
                blocks.append(bias_c[:, j + delta - a + NA_WIN_R - 1] if in_band else masked)
            q_rows.append(jnp.concatenate(blocks, axis=-1))
        tables.append(jnp.concatenate(q_rows, axis=-2))
    return jnp.stack(tables, axis=1)


def _natten(proj, bias, *, nb, heads, p_rows, lc, width, base):
    d = HEAD_DIM
    t = proj.shape[0]
    rows = (p_rows - lc) // GRID_W

    def col(k):
        return pl.BlockSpec((p_rows, d), lambda b, h: (b, (base + k) * heads + h))

    return pl.pallas_call(
        functools.partial(_natten_kernel, lc=lc, rows=rows),
        name="natten",
        grid=(nb, heads),
        in_specs=[col(0), col(1), col(2),
                  pl.BlockSpec((None,) + bias.shape[1:], lambda b, h: (h, 0, 0, 0))],
        out_specs=pl.BlockSpec((p_rows, d), lambda b, h: (b, h)),
        out_shape=jax.ShapeDtypeStruct((t, width), BF16),
        compiler_params=_cparams(("arbitrary", "arbitrary")),
    )(proj, proj, proj, bias)


def _head_norm(x, g):
    return x * lax.rsqrt(jnp.mean(x * x, axis=-1, keepdims=True) + EPS) * g


def _gqa_kernel(q_ref, k_ref, v_ref, gq_ref, gk_ref, cos_ref, sin_ref, out_ref, kn_scr, *, lc, tq):
    d = HEAD_DIM
    scale = d ** -0.5
    qt = pl.program_id(2)
    ncc = lc // tq

    @pl.when(qt == 0)
    def _():
        kn = _head_norm(k_ref[...].astype(F32), gk_ref[...])
        kn_scr[0:lc, :] = kn[0:lc].astype(BF16)
        kn_scr[lc:, :] = _rope(kn[lc:], cos_ref[...], sin_ref[...]).astype(BF16)

    def attend(rotate, n_keys):
        for g in range(Q_PER_KV):
            qn = _head_norm(q_ref[:, g * d:(g + 1) * d].astype(F32), gq_ref[...])
            if rotate:
                off = pl.multiple_of((qt - ncc) * tq, tq)
                qn = _rope(qn, cos_ref[pl.ds(off, tq), :], sin_ref[pl.ds(off, tq), :])
            qb = (qn * scale).astype(BF16)
            s = lax.dot_general(qb, kn_scr[0:n_keys, :], _NT, preferred_element_type=F32)
            out_ref[:, g * d:(g + 1) * d] = _softmax_pv([(s, v_ref[0:n_keys, :])]).astype(out_ref.dtype)

    @pl.when(qt < ncc)
    def _():
        attend(False, lc)

    @pl.when(qt >= ncc)
    def _():
        attend(True, kn_scr.shape[0])


def _gqa(proj, g_q, g_k, cosf, sinf, *, nb, heads, p_rows, lc, width):
    d = HEAD_DIM
    kvh = heads // Q_PER_KV
    t = proj.shape[0]
    tq = 256
    nq = p_rows // tq
    s_rows = cosf.shape[0]
    gw = Q_PER_KV * d
    return pl.pallas_call(
        functools.partial(_gqa_kernel, lc=lc, tq=tq),
        name="gqa",
        grid=(nb, kvh, nq),
        in_specs=[pl.BlockSpec((tq, gw), lambda b, kh, i: (b * nq + i, kh)),
                  pl.BlockSpec((p_rows, d), lambda b, kh, i: (b, heads + kh)),
                  pl.BlockSpec((p_rows, d), lambda b, kh, i: (b, heads + kvh + kh)),
                  pl.BlockSpec((1, d), lambda b, kh, i: (0, 0)),
                  pl.BlockSpec((1, d), lambda b, kh, i: (0, 0)),
                  pl.BlockSpec((s_rows, d), lambda b, kh, i: (0, 0)),
                  pl.BlockSpec((s_rows, d), lambda b, kh, i: (0, 0))],
        out_specs=pl.BlockSpec((tq, gw), lambda b, kh, i: (b * nq + i, kh)),
        out_shape=jax.ShapeDtypeStruct((t, width), BF16),
        scratch_shapes=[pltpu.VMEM((p_rows, d), BF16)],
        compiler_params=_cparams(("arbitrary", "arbitrary", "arbitrary")),
    )(proj, proj, proj, g_q.reshape(1, d), g_k.reshape(1, d), cosf, sinf)


def _retention_kernel(q_ref, k_ref, v_ref, g_ref, lg_ref, gh_ref, cos_ref, sin_ref, out_ref,
                      qs, ks, yacc, *, nc, ncc, lc):
    L = SCAN_CHUNK
    d = HEAD_DIM
    scale = d ** -0.5
    cosf = cos_ref[...]
    sinf = sin_ref[...]
    qs[0:lc, :] = q_ref[0:lc, :]
    ks[0:lc, :] = k_ref[0:lc, :]
    qs[lc:, :] = _rope(q_ref[lc:, :].astype(F32), cosf, sinf).astype(BF16)
    ks[lc:, :] = _rope(k_ref[lc:, :].astype(F32), cosf, sinf).astype(BF16)
    ii = lax.broadcasted_iota(jnp.int32, (L, L), 0)
    jj = lax.broadcasted_iota(jnp.int32, (L, L), 1)
    pos = lax.broadcasted_iota(jnp.int32, (L, 1), 0).astype(F32)
    for dirn in (0, 1):
        lg = _log_sigmoid(lg_ref[dirn:dirn + 1, 0:1])
        rel = (ii - jj) if dirn == 0 else (jj - ii)
        decay = jnp.where(rel >= 0, jnp.exp(jnp.maximum(rel, 0).astype(F32) * lg), 0.0) * scale
        if dirn == 0:
            dq = jnp.exp((pos + 1.0) * lg)
            dk = jnp.exp((L - 1.0 - pos) * lg) * scale
        else:
            dq = jnp.exp((L - pos) * lg)
            dk = jnp.exp(pos * lg) * scale
        dchunk = jnp.exp(L * lg)
        state = jnp.zeros((d, d), F32)
        for c in _chunk_order(nc, ncc, dirn == 1):
            sl = slice(c * L, (c + 1) * L)
            qc = qs[sl, :]
            kc = ks[sl, :]
            vc = v_ref[sl, :]
            s = (lax.dot_general(qc, kc, _NT, preferred_element_type=F32) * decay).astype(BF16)
            o = (jnp.dot(s, vc, preferred_element_type=F32)
                 + dq * jnp.dot(qc, state.astype(BF16), preferred_element_type=F32))
            if dirn == 0:
                yacc[sl, :] = o
            else:
                yacc[sl, :] = yacc[sl, :] + o
            kw = (kc.astype(F32) * dk).astype(BF16)
            state = dchunk * state + lax.dot_general(kw, vc, _TN, preferred_element_type=F32)
    y = yacc[...]
    yn = y * lax.rsqrt(jnp.mean(y * y, axis=-1, keepdims=True) + EPS) * gh_ref[...]
    gate = g_ref[...].astype(F32)
    out_ref[...] = (gate * _sigmoid(gate) * yn).astype(out_ref.dtype)


def _retention(proj, decay_logit, g_head, cosf, sinf, *, nb, heads, p_rows, lc, width, base):
    d = HEAD_DIM
    nc = p_rows // SCAN_CHUNK
    ncc = lc // SCAN_CHUNK
    t = proj.shape[0]
    s_rows = cosf.shape[0]
    lg = jnp.broadcast_to(decay_logit.T[:, :, None], (heads, 2, d)).astype(F32)
    lg = jnp.concatenate([lg, jnp.zeros((heads, SUBLANES - 2, d), F32)], axis=1)

    def col(k):
        return pl.BlockSpec((p_rows, d), lambda b, h: (b, base + k * heads + h))

    return pl.pallas_call(
        functools.partial(_retention_kernel, nc=nc, ncc=ncc, lc=lc),
        name="retention",
        grid=(nb, heads),
        in_specs=[col(0), col(1), col(2), col(3),
                  pl.BlockSpec((None, SUBLANES, d), lambda b, h: (h, 0, 0)),
                  pl.BlockSpec((None, 1, d), lambda b, h: (h, 0, 0)),
                  pl.BlockSpec((s_rows, d), lambda b, h: (0, 0)),
                  pl.BlockSpec((s_rows, d), lambda b, h: (0, 0))],
        out_specs=pl.BlockSpec((p_rows, d), lambda b, h: (b, h)),
        out_shape=jax.ShapeDtypeStruct((t, width), BF16),
        scratch_shapes=[pltpu.VMEM((p_rows, d), BF16), pltpu.VMEM((p_rows, d), BF16),
                        pltpu.VMEM((p_rows, d), F32)],
        compiler_params=_cparams(("arbitrary", "arbitrary")),
    )(proj, proj, proj, proj, lg, g_head.reshape(heads, 1, d), cosf, sinf)


def _outproj_kernel(a_ref, b_ref, wa_ref, wb_ref, x_ref, gt_ref, o_ref, *, sel):
    i = pl.program_id(0)
    y = (jnp.dot(a_ref[...], wa_ref[...], preferred_element_type=F32)
         + jnp.dot(b_ref[...], wb_ref[...], preferred_element_type=F32))
    o_ref[...] = x_ref[...] + _select_rows(gt_ref, i, **sel) * y


def _out_projection(mix_a, mix_b, w_a, w_b, xs, mods, *, sel, gate_chunk):
    t, d = xs.shape
    gw = mix_a.shape[1]
    tm = sel["tm"]
    tn = min(1024, d)
    mp = mods.shape[0]
    npd = d // tn
    return pl.pallas_call(
        functools.partial(_outproj_kernel, sel=sel),
        name="out_projection",
        grid=(t // tm, npd),
        in_specs=[pl.BlockSpec((tm, gw), lambda i, j: (i, 0)),
                  pl.BlockSpec((tm, gw), lambda i, j: (i, 0)),
                  pl.BlockSpec((gw, tn), lambda i, j: (0, j)),
                  pl.BlockSpec((gw, tn), lambda i, j: (0, j)),
                  pl.BlockSpec((tm, tn), lambda i, j: (i, j)),
                  pl.BlockSpec((mp, tn), lambda i, j: (0, gate_chunk * npd + j))],
        out_specs=pl.BlockSpec((tm, tn), lambda i, j: (i, j)),
        out_shape=jax.ShapeDtypeStruct((t, d), F32),
        compiler_params=_cparams(("arbitrary", "arbitrary")),
    )(mix_a, mix_b, w_a, w_b, xs, mods)


def _first_argmax(vals):
    best = vals[0]
    idx = jnp.zeros_like(best)
    for j in range(1, len(vals)):
        upd = vals[j] > best
        idx = jnp.where(upd, float(j), idx)
        best = jnp.where(upd, vals[j], best)
    return idx, best


def _pick(idx, vals):
    out = vals[-1]
    for j in range(len(vals) - 2, -1, -1):
        out = jnp.where(idx == float(j), vals[j], out)
    return out


def _ffn_pre_kernel(x_ref, g_ref, sh_ref, sc_ref, wrt_ref, br_ref, h_ref, r_ref, *, sel):
    i = pl.program_id(0)
    h = _norm_mod(x_ref[...], g_ref[...], sh_ref, sc_ref, i, **sel)
    h_ref[...] = h
    logits = lax.dot_general(wrt_ref[...], h, _NT, precision=HIGHEST, preferred_element_type=F32)
    aff = _sigmoid(logits)
    sel_s = aff + br_ref[...]
    a = [aff[e:e + 1, :] for e in range(N_EXPERTS)]
    s = [sel_s[e:e + 1, :] for e in range(N_EXPERTS)]
    n = EXPERTS_PER_GROUP
    scores = []
    for grp in range(N_GROUPS):
        v = s[grp * n:(grp + 1) * n]
        best = v[0] + v[1]
        for p in range(n):
            for q in range(p + 1, n):
                if (p, q) != (0, 1):
                    best = jnp.maximum(best, v[p] + v[q])
        scores.append(best)
    gi, _ = _first_argmax(scores)
    cv = [_pick(gi, [s[grp * n + j] for grp in range(N_GROUPS)]) for j in range(n)]
    av = [_pick(gi, [a[grp * n + j] for grp in range(N_GROUPS)]) for j in range(n)]
    i1, _ = _first_argmax(cv)
    cv2 = [jnp.where(i1 == float(j), -jnp.inf, cv[j]) for j in range(n)]
    i2, _ = _first_argmax(cv2)
    w1 = _pick(i1, av)
    w2 = _pick(i2, av)
    tot = w1 + w2
    r_ref[0:1, :] = gi * float(n) + i1
    r_ref[1:2, :] = gi * float(n) + i2
    r_ref[2:3, :] = w1 / tot
    r_ref[3:4, :] = w2 / tot
    r_ref[4:8, :] = jnp.zeros((4, r_ref.shape[1]), F32)


def _ffn_pre(xs, g, mods, w_router_t, b_router, *, sel):
    t, d = xs.shape
    tm = sel["tm"]
    mp = mods.shape[0]
    ne = w_router_t.shape[0]
    return pl.pallas_call(
        functools.partial(_ffn_pre_kernel, sel=sel),
        name="ffn_pre_router",
        grid=(t // tm,),
        in_specs=[pl.BlockSpec((tm, d), lambda i: (i, 0)),
                  pl.BlockSpec((1, d), lambda i: (0, 0)),
                  pl.BlockSpec((mp, d), lambda i: (0, 3)),
                  pl.BlockSpec((mp, d), lambda i: (0, 4)),
                  pl.BlockSpec((ne, d), lambda i: (0, 0)),
                  pl.BlockSpec((ne, 1), lambda i: (0, 0))],
        out_specs=[pl.BlockSpec((tm, d), lambda i: (i, 0)),
                   pl.BlockSpec((SUBLANES, tm), lambda i: (0, i))],
        out_shape=[jax.ShapeDtypeStruct((t, d), F32), jax.ShapeDtypeStruct((SUBLANES, t), F32)],
        compiler_params=_cparams(("arbitrary",)),
    )(xs, g.reshape(1, d), mods, mods, w_router_t, b_router.reshape(ne, 1))


def _dispatch_metadata(route, tm):
    t = route.shape[1]
    na = 2 * t
    assert na % tm == 0
    i32 = jnp.int32
    ne = N_EXPERTS
    n_pad = ne * tm
    e = route[0:2].astype(i32).reshape(na)
    expert_ids = jnp.arange(ne, dtype=i32)
    counts = jnp.sum((e[:, None] == expert_ids[None, :]).astype(i32), axis=0)
    pad_end = jnp.cumsum((-counts) % tm)
    pad_id = jnp.arange(n_pad, dtype=i32)
    pad_key = jnp.sum((pad_id[:, None] >= pad_end[None, :]).astype(i32), axis=1)
    a_id = jnp.arange(na, dtype=i32)
    keys = jnp.concatenate([e, pad_key])
    dst = jnp.concatenate([a_id, na + pad_id])
    src = jnp.concatenate([jnp.where(a_id >= t, a_id - t, a_id), jnp.zeros((n_pad,), i32)])
    keys_s, dst_s, src_s = lax.sort((keys, dst, src), num_keys=1, is_stable=True)
    te = keys_s[0::tm]
    last_e = jnp.max(jnp.where(counts > 0, expert_ids, 0))
    te = jnp.where(te < ne, te, last_e)
    n_slots = na + n_pad
    _, slot_of = lax.sort((dst_s, jnp.arange(n_slots, dtype=i32)), num_keys=1)
    zeros = jnp.zeros((tm,), i32)
    te_x = jnp.concatenate([te[:1], te, te[-1:]]).astype(i32)
    src_x = jnp.concatenate([zeros, src_s, zeros]).astype(i32)
    return te_x, src_x, slot_of[:na]


def _row_copy(src_hbm, row, dst, dst_row, sem):
    return pltpu.make_async_copy(src_hbm.at[pl.ds(row, 1), :], dst.at[pl.ds(dst_row, 1), :], sem)


def _for_rows(n, fn, unroll=8):
    def body(r, carry):
        fn(r)
        return carry
    lax.fori_loop(0, n, body, 0, unroll=unroll)


def _moe_kernel(te_ref, src_ref, h_hbm, wg_ref, wu_ref, wd_ref, y_ref, x0, x1, gsem, *, tm, n_tiles):
    del te_ref
    i = pl.program_id(0)
    xbufs = (x0, x1)

    def gather_copy(entry, r, buf, sem):
        return _row_copy(h_hbm, src_ref[entry * tm + r], buf, r, sem)

    @pl.when(i == 0)
    def _():
        _for_rows(tm, lambda r: gather_copy(1, r, x0, gsem.at[0]).start())

    for par in (0, 1):
        @pl.when(jnp.logical_and(jnp.logical_and(i >= 1, i <= n_tiles), (i - 1) % 2 == par))
        def _(cur_x=xbufs[par], nxt_x=xbufs[1 - par], par=par):
            _for_rows(tm, lambda r: gather_copy(0, r, cur_x, gsem.at[par]).wait())
            for r in range(tm):
                gather_copy(i + 1, r, nxt_x, gsem.at[1 - par]).start()
            xb = cur_x[...].astype(BF16)
            g = jnp.dot(xb, wg_ref[...], preferred_element_type=F32)
            u = jnp.dot(xb, wu_ref[...], preferred_element_type=F32)
            act = (g * _sigmoid(g) * u).astype(BF16)
            y_ref[...] = jnp.dot(act, wd_ref[...], preferred_element_type=F32)

    @pl.when(i == n_tiles + 1)
    def _():
        p_dummy = n_tiles % 2
        _for_rows(tm, lambda r: gather_copy(0, r, xbufs[p_dummy], gsem.at[p_dummy]).wait())


def _moe_experts(h2, te_x, src_x, wg, wu, wd):
    t, d = h2.shape
    tm = MOE_TILE
    n_tiles = te_x.shape[0] - 2
    ff = wg.shape[2]
    grid_spec = pltpu.PrefetchScalarGridSpec(
        num_scalar_prefetch=2,
        grid=(n_tiles + 2,),
        in_specs=[pl.BlockSpec(memory_space=pl.ANY),
                  pl.BlockSpec((None, d, ff), lambda i, te, src: (te[i], 0, 0)),
                  pl.BlockSpec((None, d, ff), lambda i, te, src: (te[i], 0, 0)),
                  pl.BlockSpec((None, ff, d), lambda i, te, src: (te[i], 0, 0))],
        out_specs=pl.BlockSpec((tm, d), lambda i, te, src: (jnp.clip(i - 1, 0, n_tiles - 1), 0)),
        scratch_shapes=[pltpu.VMEM((tm, d), F32), pltpu.VMEM((tm, d), F32), pltpu.SemaphoreType.DMA((2,))],
    )
    return pl.pallas_call(
        functools.partial(_moe_kernel, tm=tm, n_tiles=n_tiles),
        name="moe_experts",
        grid_spec=grid_spec,
        out_shape=jax.ShapeDtypeStruct((n_tiles * tm, d), F32),
        compiler_params=_cparams(("arbitrary",)),
    )(te_x, src_x, h2, wg, wu, wd)


def _combine_kernel(slot_ref, y_hbm, w0_ref, w1_ref, x_ref, gt_ref, o_ref, ybuf, sem, *, tm, nt, t_total, sel):
    i = pl.program_id(0)
    buf = i % 2

    def row_pair(tile, r, b, action):
        for k in range(2):
            copy = _row_copy(y_hbm, slot_ref[k * t_total + tile * tm + r], ybuf.at[b, k], r, sem.at[b])
            action(copy)

    @pl.when(i == 0)
    def _():
        _for_rows(tm, lambda r: row_pair(0, r, 0, lambda c: c.start()), unroll=4)

    @pl.when(i + 1 < nt)
    def _():
        _for_rows(tm, lambda r: row_pair(i + 1, r, 1 - buf, lambda c: c.start()), unroll=4)

    _for_rows(tm, lambda r: row_pair(0, r, buf, lambda c: c.wait()), unroll=4)
    mix = w0_ref[...] * ybuf[buf, 0] + w1_ref[...] * ybuf[buf, 1]
    o_ref[...] = x_ref[...] + _select_rows(gt_ref, i, **sel) * mix


def _moe_combine(ys, slots, route, xs, mods, *, sel):
    t, d = xs.shape
    tm = sel["tm"]
    nt = t // tm
    mp = mods.shape[0]
    w0 = route[2].reshape(t, 1)
    w1 = route[3].reshape(t, 1)
    grid_spec = pltpu.PrefetchScalarGridSpec(
        num_scalar_prefetch=1,
        grid=(nt,),
        in_specs=[pl.BlockSpec(memory_space=pl.ANY),
                  pl.BlockSpec((tm, 1), lambda i, sl: (i, 0)),
                  pl.BlockSpec((tm, 1), lambda i, sl: (i, 0)),
                  pl.BlockSpec((tm, d), lambda i, sl: (i, 0)),
                  pl.BlockSpec((mp, d), lambda i, sl: (0, 5))],
        out_specs=pl.BlockSpec((tm, d), lambda i, sl: (i, 0)),
        scratch_shapes=[pltpu.VMEM((2, 2, tm, d), F32), pltpu.SemaphoreType.DMA((2,))],
    )
    return pl.pallas_call(
        functools.partial(_combine_kernel, tm=tm, nt=nt, t_total=t, sel=sel),
        name="moe_combine",
        grid_spec=grid_spec,
        out_shape=jax.ShapeDtypeStruct((t, d), F32),
        compiler_params=_cparams(("arbitrary",)),
    )(slots, ys, w0, w1, xs, mods)


def _final_norm_kernel(x_ref, g_ref, o_ref):
    x = x_ref[...]
    o_ref[...] = x * lax.rsqrt(jnp.mean(x * x, axis=-1, keepdims=True) + EPS) * g_ref[...]


def _final_norm(xs, g, *, nb, p_rows, lc):
    t, d = xs.shape
    tm = 256
    tpb = p_rows // tm
    ncc = lc // tm
    lat = tpb - ncc
    return pl.pallas_call(
        _final_norm_kernel,
        name="final_norm",
        grid=(nb, lat),
        in_specs=[pl.BlockSpec((tm, d), lambda b, j: (b * tpb + ncc + j, 0)),
                  pl.BlockSpec((1, d), lambda b, j: (0, 0))],
        out_specs=pl.BlockSpec((tm, d), lambda b, j: (b * lat + j, 0)),
        out_shape=jax.ShapeDtypeStruct((nb * lat * tm, d), F32),
        compiler_params=_cparams(("arbitrary", "arbitrary")),
    )(xs, g.reshape(1, d))


def _rope_tables(n_tokens):
    tpos = np.arange(n_tokens)
    row = (tpos // GRID_W).astype(np.float32)
    col = (tpos % GRID_W).astype(np.float32)
    n_axis = HEAD_DIM // 4
    inv = jnp.asarray(ROPE_BASE, F32) ** (-jnp.arange(n_axis, dtype=F32) / n_axis)
    ang = jnp.concatenate([jnp.asarray(row)[:, None] * inv, jnp.asarray(col)[:, None] * inv], axis=-1)
    cos, sin = jnp.cos(ang), jnp.sin(ang)
    return jnp.concatenate([cos, cos], axis=-1), jnp.concatenate([-sin, sin], axis=-1)


def kernel(x, c, ctx, c_ctx, w_mod, b_mod, g_norm_mix, g_norm_ffn, g_norm_out, w_in_ab, b_gate_ab, g_mlstm, rpb_na, w_out_ab, w_in_cd, g_qnorm, g_knorm, ret_decay_logit, g_ret, w_out_cd, w_router, b_router, w_exp_gate, w_exp_up, w_exp_down):
    nb, s_len, d = x.shape
    lc = ctx.shape[1]
    p_rows = lc + s_len
    heads = d // (2 * HEAD_DIM)
    kvh = heads // Q_PER_KV
    gw = heads * HEAD_DIM
    depth = w_mod.shape[0]
    assert lc % SCAN_CHUNK == 0 and s_len % SCAN_CHUNK == 0 and heads % Q_PER_KV == 0

    xs = jnp.concatenate([ctx, x], axis=1).reshape(nb * p_rows, d)
    mp = -(-(nb + 1) // SUBLANES) * SUBLANES
    cc = jnp.concatenate([c, c_ctx[None, :], jnp.zeros((mp - nb - 1, d), F32)], axis=0)
    mods = _modulation(cc, w_mod, b_mod)

    tm = _row_tile(p_rows)
    sel = dict(lc=lc, tm=tm, tpb=p_rows // tm, nb=nb)
    sel_c = dict(lc=lc, tm=COMBINE_TILE, tpb=p_rows // COMBINE_TILE, nb=nb)
    cosf, sinf = _rope_tables(s_len)
    nc = p_rows // SCAN_CHUNK
    ncp = -(-nc // SUBLANES) * SUBLANES
    seq = dict(nb=nb, heads=heads, p_rows=p_rows, lc=lc, width=gw)
    w_router_t = w_router.T.astype(F32)

    for layer in range(depth):
        ml = mods[layer]
        p = layer // 2
        if layer % 2 == 0:
            w = w_in_ab[p]
            ng = 4 * heads
            w_main = jnp.concatenate([w[:, :4 * gw], w[:, 4 * gw + ng:]], axis=1).astype(BF16)
            w_gate_t = w[:, 4 * gw:4 * gw + ng].T.astype(BF16)
            proj, gates_t = _in_projection(xs, g_norm_mix[layer], ml, w_main, w_gate_t, b_gate_ab[p], sel=sel)
            gates = gates_t.reshape(4, heads, nb, nc, SCAN_CHUNK)
            gates = jnp.pad(gates, ((0, 0), (0, 0), (0, 0), (0, ncp - nc), (0, 0)))
            mix_a = _mlstm(proj, gates, g_mlstm[p], **seq)
            mix_b = _natten(proj, _natten_bias(rpb_na[p], s_len // GRID_W), base=4, **seq)
            w_out = w_out_ab[p]
        else:
            proj = _in_projection(xs, g_norm_mix[layer], ml, w_in_cd[p].astype(BF16), None, None, sel=sel)
            mix_a = _gqa(proj, g_qnorm[p], g_knorm[p], cosf, sinf, **seq)
            mix_b = _retention(proj, ret_decay_logit[p], g_ret[p], cosf, sinf, base=heads + 2 * kvh, **seq)
            w_out = w_out_cd[p]
        xs = _out_projection(mix_a, mix_b, w_out[:gw].astype(BF16), w_out[gw:].astype(BF16), xs, ml,
                             sel=sel, gate_chunk=2)
        h2, route = _ffn_pre(xs, g_norm_ffn[layer], ml, w_router_t, b_router, sel=sel)
        te_x, src_x, slots = _dispatch_metadata(route, MOE_TILE)
        ys = _moe_experts(h2, te_x, src_x, w_exp_gate[layer].astype(BF16),
                          w_exp_up[layer].astype(BF16), w_exp_down[layer].astype(BF16))
        xs = _moe_combine(ys, slots, route, xs, ml, sel=sel_c)
    return _final_norm(xs, g_norm_out, nb=nb, p_rows=p_rows, lc=lc).reshape(nb, s_len, d)
```

```python
import functools

import jax
import jax.numpy as jnp
import numpy as np
from jax import lax
from jax.experimental import pallas as pl
from jax.experimental.pallas import tpu as pltpu

F32 = jnp.float32
BF16 = jnp.bfloat16
HIGHEST = lax.Precision.HIGHEST

HEAD_DIM = 128
GRID_W = 64
NA_WIN_R = 8
NA_WIN_C = 16
ROPE_BASE = 10000.0
N_EXPERTS = 16
N_GROUPS = 4
EXPERTS_PER_GROUP = N_EXPERTS // N_GROUPS
N_MOD = 6
EPS = 1e-6
Q_PER_KV = 4
SCAN_CHUNK = 256
LANES = 128
SUBLANES = 8
MOE_TILE = 512
MAX_COL_TILE = 1536
COMBINE_TILE = 256
NA_Q_ROWS = 4
NA_K_ROWS = NA_Q_ROWS + NA_WIN_R
NEG_BIG = -1e30
VMEM_LIMIT = 56 * 1024 * 1024

_NT = (((1,), (1,)), ((), ()))
_TN = (((0,), (0,)), ((), ()))


def _cparams(sem):
    return pltpu.CompilerParams(dimension_semantics=sem, vmem_limit_bytes=VMEM_LIMIT)


def _sigmoid(x):
    return 1.0 / (1.0 + jnp.exp(-x))


def _log_sigmoid(x):
    return jnp.minimum(x, 0.0) - jnp.log(1.0 + jnp.exp(-jnp.abs(x)))


def _row_tile(p_rows):
    best = 256
    for t in (512, 768):
        if p_rows % t == 0:
            best = t
    return best


def _col_tile(n_cols):
    assert n_cols % LANES == 0
    return max(t for t in range(LANES, MAX_COL_TILE + 1, LANES) if n_cols % t == 0)


def _select_rows(ref, i, *, lc, tm, tpb, nb):
    b = i // tpb
    j = i % tpb
    per_sample = ref[pl.ds(b, 1), :]
    ctx = ref[nb:nb + 1, :]
    pos = j * tm + lax.broadcasted_iota(jnp.int32, (tm, 1), 0)
    return jnp.where(pos < lc, ctx, per_sample)


def _norm_mod(x, g, sh_ref, sc_ref, i, **kw):
    ms = jnp.mean(x * x, axis=-1, keepdims=True)
    y = x * lax.rsqrt(ms + EPS) * g
    return y * (1.0 + _select_rows(sc_ref, i, **kw)) + _select_rows(sh_ref, i, **kw)


def _rope(x, cosf, sinf):
    return x * cosf + pltpu.roll(x, HEAD_DIM // 2, 1) * sinf


def _mod_kernel(cc_ref, w_ref, b_ref, o_ref):
    a = cc_ref[...]
    a = a * _sigmoid(a)
    o_ref[0] = jnp.dot(a.astype(BF16), w_ref[0].astype(BF16), preferred_element_type=F32) + b_ref[0]


def _modulation(cc, w_mod, b_mod):
    depth, d, n = w_mod.shape
    mp = cc.shape[0]
    tn = 1024
    return pl.pallas_call(
        _mod_kernel,
        name="modulation",
        grid=(depth, n // tn),
        in_specs=[
            pl.BlockSpec((mp, d), lambda l, j: (0, 0)),
            pl.BlockSpec((1, d, tn), lambda l, j: (l, 0, j)),
            pl.BlockSpec((1, 1, tn), lambda l, j: (l, 0, j)),
        ],
        out_specs=pl.BlockSpec((1, mp, tn), lambda l, j: (l, 0, j)),
        out_shape=jax.ShapeDtypeStruct((depth, mp, n), F32),
        compiler_params=_cparams(("arbitrary", "arbitrary")),
    )(cc, w_mod, b_mod.reshape(depth, 1, n))


def _inproj_kernel(*refs, with_gates, sel):
    if with_gates:
        x_ref, g_ref, sh_ref, sc_ref, w_ref, wgt_ref, bg_ref, o_ref, gt_ref, h_scr = refs
    else:
        x_ref, g_ref, sh_ref, sc_ref, w_ref, o_ref, h_scr = refs
    i = pl.program_id(0)

    @pl.when(pl.program_id(1) == 0)
    def _():
        h = _norm_mod(x_ref[...], g_ref[...], sh_ref, sc_ref, i, **sel)
        h_scr[...] = h.astype(BF16)
        if with_gates:
            gt_ref[...] = lax.dot_general(wgt_ref[...], h_scr[...], _NT, preferred_element_type=F32) + bg_ref[...]

    o_ref[...] = jnp.dot(h_scr[...], w_ref[...], preferred_element_type=F32).astype(o_ref.dtype)


def _in_projection(xs, g, mods, w, wgt, bg, *, sel):
    t, d = xs.shape
    n = w.shape[1]
    tn = _col_tile(n)
    tm = sel["tm"]
    mp = mods.shape[0]
    with_gates = wgt is not None
    in_specs = [
        pl.BlockSpec((tm, d), lambda i, j: (i, 0)),
        pl.BlockSpec((1, d), lambda i, j: (0, 0)),
        pl.BlockSpec((mp, d), lambda i, j: (0, 0)),
        pl.BlockSpec((mp, d), lambda i, j: (0, 1)),
        pl.BlockSpec((d, tn), lambda i, j: (0, j)),
    ]
    out_specs = [pl.BlockSpec((tm, tn), lambda i, j: (i, j))]
    out_shape = [jax.ShapeDtypeStruct((t, n), BF16)]
    args = [xs, g.reshape(1, d), mods, mods, w]
    if with_gates:
        ng = wgt.shape[0]
        in_specs += [pl.BlockSpec((ng, d), lambda i, j: (0, 0)), pl.BlockSpec((ng, 1), lambda i, j: (0, 0))]
        out_specs.append(pl.BlockSpec((ng, tm), lambda i, j: (0, i)))
        out_shape.append(jax.ShapeDtypeStruct((ng, t), F32))
        args += [wgt, bg.reshape(ng, 1)]
    res = pl.pallas_call(
        functools.partial(_inproj_kernel, with_gates=with_gates, sel=sel),
        name="in_projection",
        grid=(t // tm, n // tn),
        in_specs=in_specs,
        out_specs=out_specs,
        out_shape=out_shape,
        scratch_shapes=[pltpu.VMEM((tm, d), BF16)],
        compiler_params=_cparams(("arbitrary", "arbitrary")),
    )(*args)
    return res if with_gates else res[0]


def _chunk_order(nc, ncc, reverse):
    ctx = list(range(ncc))
    lat = list(range(ncc, nc))
    return (ctx[::-1] + lat[::-1]) if reverse else (ctx + lat)


def _mlstm_kernel(q_ref, k_ref, v_ref, o_ref, gates_ref, gh_ref, out_ref, vaug, yacc, *, nc, ncc):
    L = SCAN_CHUNK
    d = HEAD_DIM
    scale = d ** -0.5
    p_rows = nc * L
    vaug[:, 0:d] = v_ref[...]
    lane = lax.broadcasted_iota(jnp.int32, (p_rows, d), 1)
    vaug[:, d:2 * d] = jnp.where(lane == 0, 1.0, 0.0).astype(BF16)
    gates = gates_ref[...]
    ii = lax.broadcasted_iota(jnp.int32, (L, L), 0)
    jj = lax.broadcasted_iota(jnp.int32, (L, L), 1)
    eye = (ii == jj).astype(F32)
    lane_row = lax.broadcasted_iota(jnp.int32, (gates.shape[1], L), 1)

    def to_col(rows):
        return lax.dot_general(eye, rows, _NT, precision=HIGHEST, preferred_element_type=F32)

    def running_max(x, reverse):
        k = 1
        while k < L:
            if reverse:
                shifted = jnp.where(lane_row < L - k, pltpu.roll(x, L - k, 1), -jnp.inf)
            else:
                shifted = jnp.where(lane_row >= k, pltpu.roll(x, k, 1), -jnp.inf)
            x = jnp.maximum(x, shifted)
            k *= 2
        return x

    pre = []
    for dirn in (0, 1):
        li = gates[2 * dirn]
        lf = _log_sigmoid(gates[2 * dirn + 1])
        mask = (jj <= ii) if dirn == 0 else (jj >= ii)
        mask_f = mask.astype(F32)
        b_row = lax.dot_general(lf, mask_f, _NT, precision=HIGHEST, preferred_element_type=F32)
        b_col = lax.dot_general(mask_f, lf, _NT, precision=HIGHEST, preferred_element_type=F32)
        a_row = li - b_row
        amax_row = running_max(a_row, dirn == 1)
        last = slice(L - 1, L) if dirn == 0 else slice(0, 1)
        pre.append(dict(mask=mask, a_row=a_row, a_col=to_col(li) - b_col, b_col=b_col, amax_col=to_col(amax_row),
                        b_end=b_row[:, last], amax_end=amax_row[:, last]))
    state = [(jnp.zeros((d, 2 * d), F32), jnp.zeros((1, 1), F32)) for _ in (0, 1)]
    orders = [_chunk_order(nc, ncc, False), _chunk_order(nc, ncc, True)]
    written = set()
    for step in range(nc):
        for dirn in (0, 1):
            c = orders[dirn][step]
            p = pre[dirn]
            c_state, m = state[dirn]
            sl = slice(c * L, (c + 1) * L)
            qc = q_ref[sl, :]
            kc = k_ref[sl, :]
            va = vaug[sl, :]
            b_end = p["b_end"][c:c + 1, :]
            amax_end = p["amax_end"][c:c + 1, :]
            mx = jnp.maximum(m, p["amax_col"][:, c:c + 1])
            w = jnp.exp(jnp.where(p["mask"], p["a_row"][c:c + 1, :] - mx, -jnp.inf)) * scale
            w_inter = jnp.exp(m - mx)
            qk = lax.dot_general(qc, kc, _NT, preferred_element_type=F32)
            s = (qk * w).astype(BF16)
            r = jnp.dot(s, va, preferred_element_type=F32)
            if step > 0:
                r = r + w_inter * jnp.dot(qc, c_state.astype(BF16), preferred_element_type=F32)
            h = r[:, 0:d] / jnp.maximum(jnp.abs(r[:, d:d + 1]), jnp.exp(-(p["b_col"][:, c:c + 1] + mx)))
            if c in written:
                yacc[sl, :] = yacc[sl, :] + h
            else:
                yacc[sl, :] = h
                written.add(c)
            if step == nc - 1:
                continue
            m_end = jnp.maximum(m, amax_end)
            w_end = jnp.exp(p["a_col"][:, c:c + 1] - m_end) * scale
            w_prev = jnp.exp(m - m_end)
            kw = (kc.astype(F32) * w_end).astype(BF16)
            c_state = w_prev * c_state + lax.dot_general(kw, va, _TN, preferred_element_type=F32)
            state[dirn] = (c_state, b_end + m_end)
    y = yacc[...]
    yn = y * lax.rsqrt(jnp.mean(y * y, axis=-1, keepdims=True) + EPS) * gh_ref[...]
    out_ref[...] = (_sigmoid(o_ref[...].astype(F32)) * yn).astype(out_ref.dtype)


def _mlstm(proj, gates, g_head, *, nb, heads, p_rows, lc, width):
    nc = p_rows // SCAN_CHUNK
    ncc = lc // SCAN_CHUNK
    ncp = gates.shape[3]
    d = HEAD_DIM
    t = proj.shape[0]

    def col(base):
        return pl.BlockSpec((p_rows, d), lambda b, h: (b, base * heads + h))

    return pl.pallas_call(
        functools.partial(_mlstm_kernel, nc=nc, ncc=ncc),
        name="mlstm",
        grid=(nb, heads),
        in_specs=[col(0), col(1), col(2), col(3),
                  pl.BlockSpec((4, None, None, ncp, SCAN_CHUNK), lambda b, h: (0, h, b, 0, 0)),
                  pl.BlockSpec((None, 1, d), lambda b, h: (h, 0, 0))],
        out_specs=pl.BlockSpec((p_rows, d), lambda b, h: (b, h)),
        out_shape=jax.ShapeDtypeStruct((t, width), BF16),
        scratch_shapes=[pltpu.VMEM((p_rows, 2 * d), BF16), pltpu.VMEM((p_rows, d), F32)],
        compiler_params=_cparams(("arbitrary", "arbitrary")),
    )(proj, proj, proj, proj, gates, g_head.reshape(heads, 1, d))


def _softmax_pv(parts):
    m = parts[0][0].max(axis=-1, keepdims=True)
    for s, _ in parts[1:]:
        m = jnp.maximum(m, s.max(axis=-1, keepdims=True))
    num = 0.0
    den = 0.0
    for s, v in parts:
        p = jnp.exp(s - m)
        den = den + jnp.sum(p, axis=-1, keepdims=True)
        num = num + jnp.dot(p.astype(BF16), v, preferred_element_type=F32)
    return num / den


def _natten_kernel(q_ref, k_ref, v_ref, bias_ref, out_ref, *, lc, rows):
    scale = HEAD_DIM ** -0.5
    kc = k_ref[0:lc, :]
    vc = v_ref[0:lc, :]
    s_cc = lax.dot_general(q_ref[0:lc, :], kc, _NT, preferred_element_type=F32) * scale
    out_ref[0:lc, :] = _softmax_pv([(s_cc, vc)]).astype(out_ref.dtype)
    n_blocks = rows // NA_Q_ROWS
    for rb in range(n_blocks):
        r = rb * NA_Q_ROWS
        ks = min(max(r - NA_WIN_R // 2, 0), rows - NA_K_ROWS)
        pattern = 0 if rb == 0 else (2 if rb == n_blocks - 1 else 1)
        q_sl = slice(lc + r * GRID_W, lc + (r + NA_Q_ROWS) * GRID_W)
        k_sl = slice(lc + ks * GRID_W, lc + (ks + NA_K_ROWS) * GRID_W)
        qr = q_ref[q_sl, :]
        kb = k_ref[k_sl, :]
        vb = v_ref[k_sl, :]
        s_lat = lax.dot_general(qr, kb, _NT, preferred_element_type=F32) * scale + bias_ref[pattern]
        s_ctx = lax.dot_general(qr, kc, _NT, preferred_element_type=F32) * scale
        out_ref[q_sl, :] = _softmax_pv([(s_lat, vb), (s_ctx, vc)]).astype(out_ref.dtype)


def _natten_bias(rpb, rows):
    assert rows % NA_Q_ROWS == 0 and rows >= NA_K_ROWS
    col = np.arange(GRID_W)
    c0 = np.clip(col - NA_WIN_C // 2, 0, GRID_W - NA_WIN_C)
    col_ok = (col[None, :] >= c0[:, None]) & (col[None, :] < c0[:, None] + NA_WIN_C)
    dc_idx = np.clip(col[None, :] - col[:, None], -(NA_WIN_C - 1), NA_WIN_C - 1) + NA_WIN_C - 1
    bias_c = jnp.where(col_ok[None, None], rpb[:, :, dc_idx], NEG_BIG)
    masked = jnp.full(bias_c[:, 0].shape, NEG_BIG, F32)
    half = NA_WIN_R // 2
    patterns = [(0, lambda a: 0), (-half, lambda a: a), (-NA_WIN_R, lambda a: half)]
    tables = []
    for delta, band0 in patterns:
        q_rows = []
        for a in range(NA_Q_ROWS):
            blocks = []
            for j in range(NA_K_ROWS):
                in_band = band0(a) <= j < band0(a) + NA_WIN_R
                blocks.append(bias_c[:, j + delta - a + NA_WIN_R - 1] if in_band else masked)
            q_rows.append(jnp.concatenate(blocks, axis=-1))
        tables.append(jnp.concatenate(q_rows, axis=-2))
    return jnp.stack(tables, axis=1)


def _natten(proj, bias, *, nb, heads, p_rows, lc, width, base):
    d = HEAD_DIM
    t = proj.shape[0]
    rows = (p_rows - lc) // GRID_W

    def col(k):
        return pl.BlockSpec((p_rows, d), lambda b, h: (b, (base + k) * heads + h))

    return pl.pallas_call(
        functools.partial(_natten_kernel, lc=lc, rows=rows),
        name="natten",
        grid=(nb, heads),
        in_specs=[col(0), col(1), col(2),
                  pl.BlockSpec((None,) + bias.shape[1:], lambda b, h: (h, 0, 0, 0))],
        out_specs=pl.BlockSpec((p_rows, d), lambda b, h: (b, h)),
        out_shape=jax.ShapeDtypeStruct((t, width), BF16),
        compiler_params=_cparams(("arbitrary", "arbitrary")),
    )(proj, proj, proj, bias)


def _head_norm(x, g):
    return x * lax.rsqrt(jnp.mean(x * x, axis=-1, keepdims=True) + EPS) * g


def _gqa_kernel(q_ref, k_ref, v_ref, gq_ref, gk_ref, cos_ref, sin_ref, out_ref, kn_scr, *, lc, tq):
    d = HEAD_DIM
    scale = d ** -0.5
    qt = pl.program_id(2)
    ncc = lc // tq

    @pl.when(qt == 0)
    def _():
        kn = _head_norm(k_ref[...].astype(F32), gk_ref[...])
        kn_scr[0:lc, :] = kn[0:lc].astype(BF16)
        kn_scr[lc:, :] = _rope(kn[lc:], cos_ref[...], sin_ref[...]).astype(BF16)

    def attend(rotate, n_keys):
        for g in range(Q_PER_KV):
            qn = _head_norm(q_ref[:, g * d:(g + 1) * d].astype(F32), gq_ref[...])
            if rotate:
                off = pl.multiple_of((qt - ncc) * tq, tq)
                qn = _rope(qn, cos_ref[pl.ds(off, tq), :], sin_ref[pl.ds(off, tq), :])
            qb = (qn * scale).astype(BF16)
            s = lax.dot_general(qb, kn_scr[0:n_keys, :], _NT, preferred_element_type=F32)
            out_ref[:, g * d:(g + 1) * d] = _softmax_pv([(s, v_ref[0:n_keys, :])]).astype(out_ref.dtype)

    @pl.when(qt < ncc)
    def _():
        attend(False, lc)

    @pl.when(qt >= ncc)
    def _():
        attend(True, kn_scr.shape[0])


def _gqa(proj, g_q, g_k, cosf, sinf, *, nb, heads, p_rows, lc, width):
    d = HEAD_DIM
    kvh = heads // Q_PER_KV
    t = proj.shape[0]
    tq = 256
    nq = p_rows // tq
    s_rows = cosf.shape[0]
    gw = Q_PER_KV * d
    return pl.pallas_call(
        functools.partial(_gqa_kernel, lc=lc, tq=tq),
        name="gqa",
        grid=(nb, kvh, nq),
        in_specs=[pl.BlockSpec((tq, gw), lambda b, kh, i: (b * nq + i, kh)),
                  pl.BlockSpec((p_rows, d), lambda b, kh, i: (b, heads + kh)),
                  pl.BlockSpec((p_rows, d), lambda b, kh, i: (b, heads + kvh + kh)),
                  pl.BlockSpec((1, d), lambda b, kh, i: (0, 0)),
                  pl.BlockSpec((1, d), lambda b, kh, i: (0, 0)),
                  pl.BlockSpec((s_rows, d), lambda b, kh, i: (0, 0)),
                  pl.BlockSpec((s_rows, d), lambda b, kh, i: (0, 0))],
        out_specs=pl.BlockSpec((tq, gw), lambda b, kh, i: (b * nq + i, kh)),
        out_shape=jax.ShapeDtypeStruct((t, width), BF16),
        scratch_shapes=[pltpu.VMEM((p_rows, d), BF16)],
        compiler_params=_cparams(("arbitrary", "arbitrary", "arbitrary")),
    )(proj, proj, proj, g_q.reshape(1, d), g_k.reshape(1, d), cosf, sinf)


def _retention_kernel(q_ref, k_ref, v_ref, g_ref, lg_ref, gh_ref, cos_ref, sin_ref, out_ref,
                      qs, ks, yacc, *, nc, ncc, lc):
    L = SCAN_CHUNK
    d = HEAD_DIM
    scale = d ** -0.5
    cosf = cos_ref[...]
    sinf = sin_ref[...]
    qs[0:lc, :] = q_ref[0:lc, :]
    ks[0:lc, :] = k_ref[0:lc, :]
    qs[lc:, :] = _rope(q_ref[lc:, :].astype(F32), cosf, sinf).astype(BF16)
    ks[lc:, :] = _rope(k_ref[lc:, :].astype(F32), cosf, sinf).astype(BF16)
    ii = lax.broadcasted_iota(jnp.int32, (L, L), 0)
    jj = lax.broadcasted_iota(jnp.int32, (L, L), 1)
    pos = lax.broadcasted_iota(jnp.int32, (L, 1), 0).astype(F32)
    decay_sum = 0.0
    dq, dk, dchunk = [], [], []
    for dirn in (0, 1):
        lg = _log_sigmoid(lg_ref[dirn:dirn + 1, 0:1])
        rel = (ii - jj) if dirn == 0 else (jj - ii)
        decay_sum = decay_sum + jnp.where(rel >= 0, jnp.exp(jnp.maximum(rel, 0).astype(F32) * lg), 0.0)
        if dirn == 0:
            dq.append(jnp.exp((pos + 1.0) * lg))
            dk.append(jnp.exp((L - 1.0 - pos) * lg) * scale)
        else:
            dq.append(jnp.exp((L - pos) * lg))
            dk.append(jnp.exp(pos * lg) * scale)
        dchunk.append(jnp.exp(L * lg))
    decay_sum = decay_sum * scale
    for c in range(nc):
        sl = slice(c * L, (c + 1) * L)
        s = (lax.dot_general(qs[sl, :], ks[sl, :], _NT, preferred_element_type=F32) * decay_sum).astype(BF16)
        yacc[sl, :] = jnp.dot(s, v_ref[sl, :], preferred_element_type=F32)
    state = [jnp.zeros((d, d), F32), jnp.zeros((d, d), F32)]
    orders = [_chunk_order(nc, ncc, False), _chunk_order(nc, ncc, True)]
    for step in range(nc):
        for dirn in (0, 1):
            c = orders[dirn][step]
            sl = slice(c * L, (c + 1) * L)
            kc = ks[sl, :]
            vc = v_ref[sl, :]
            if step > 0:
                inter = jnp.dot(qs[sl, :], state[dirn].astype(BF16), preferred_element_type=F32)
                yacc[sl, :] = yacc[sl, :] + dq[dirn] * inter
            if step < nc - 1:
                kw = (kc.astype(F32) * dk[dirn]).astype(BF16)
                state[dirn] = dchunk[dirn] * state[dirn] + lax.dot_general(kw, vc, _TN, preferred_element_type=F32)
    y = yacc[...]
    yn = y * lax.rsqrt(jnp.mean(y * y, axis=-1, keepdims=True) + EPS) * gh_ref[...]
    gate = g_ref[...].astype(F32)
    out_ref[...] = (gate * _sigmoid(gate) * yn).astype(out_ref.dtype)


def _retention(proj, decay_logit, g_head, cosf, sinf, *, nb, heads, p_rows, lc, width, base):
    d = HEAD_DIM
    nc = p_rows // SCAN_CHUNK
    ncc = lc // SCAN_CHUNK
    t = proj.shape[0]
    s_rows = cosf.shape[0]
    lg = jnp.broadcast_to(decay_logit.T[:, :, None], (heads, 2, d)).astype(F32)
    lg = jnp.concatenate([lg, jnp.zeros((heads, SUBLANES - 2, d), F32)], axis=1)

    def col(k):
        return pl.BlockSpec((p_rows, d), lambda b, h: (b, base + k * heads + h))

    return pl.pallas_call(
        functools.partial(_retention_kernel, nc=nc, ncc=ncc, lc=lc),
        name="retention",
        grid=(nb, heads),
        in_specs=[col(0), col(1), col(2), col(3),
                  pl.BlockSpec((None, SUBLANES, d), lambda b, h: (h, 0, 0)),
                  pl.BlockSpec((None, 1, d), lambda b, h: (h, 0, 0)),
                  pl.BlockSpec((s_rows, d), lambda b, h: (0, 0)),
                  pl.BlockSpec((s_rows, d), lambda b, h: (0, 0))],
        out_specs=pl.BlockSpec((p_rows, d), lambda b, h: (b, h)),
        out_shape=jax.ShapeDtypeStruct((t, width), BF16),
        scratch_shapes=[pltpu.VMEM((p_rows, d), BF16), pltpu.VMEM((p_rows, d), BF16),
                        pltpu.VMEM((p_rows, d), F32)],
        compiler_params=_cparams(("arbitrary", "arbitrary")),
    )(proj, proj, proj, proj, lg, g_head.reshape(heads, 1, d), cosf, sinf)


def _outproj_kernel(a_ref, b_ref, wa_ref, wb_ref, x_ref, gt_ref, o_ref, *, sel):
    i = pl.program_id(0)
    y = (jnp.dot(a_ref[...], wa_ref[...], preferred_element_type=F32)
         + jnp.dot(b_ref[...], wb_ref[...], preferred_element_type=F32))
    o_ref[...] = x_ref[...] + _select_rows(gt_ref, i, **sel) * y


def _out_projection(mix_a, mix_b, w_a, w_b, xs, mods, *, sel, gate_chunk):
    t, d = xs.shape
    gw = mix_a.shape[1]
    tm = sel["tm"]
    tn = min(1024, d)
    mp = mods.shape[0]
    npd = d // tn
    return pl.pallas_call(
        functools.partial(_outproj_kernel, sel=sel),
        name="out_projection",
        grid=(t // tm, npd),
        in_specs=[pl.BlockSpec((tm, gw), lambda i, j: (i, 0)),
                  pl.BlockSpec((tm, gw), lambda i, j: (i, 0)),
                  pl.BlockSpec((gw, tn), lambda i, j: (0, j)),
                  pl.BlockSpec((gw, tn), lambda i, j: (0, j)),
                  pl.BlockSpec((tm, tn), lambda i, j: (i, j)),
                  pl.BlockSpec((mp, tn), lambda i, j: (0, gate_chunk * npd + j))],
        out_specs=pl.BlockSpec((tm, tn), lambda i, j: (i, j)),
        out_shape=jax.ShapeDtypeStruct((t, d), F32),
        compiler_params=_cparams(("arbitrary", "arbitrary")),
    )(mix_a, mix_b, w_a, w_b, xs, mods)


def _first_argmax(vals):
    best = vals[0]
    idx = jnp.zeros_like(best)
    for j in range(1, len(vals)):
        upd = vals[j] > best
        idx = jnp.where(upd, float(j), idx)
        best = jnp.where(upd, vals[j], best)
    return idx, best


def _pick(idx, vals):
    out = vals[-1]
    for j in range(len(vals) - 2, -1, -1):
        out = jnp.where(idx == float(j), vals[j], out)
    return out


def _ffn_pre_kernel(x_ref, g_ref, sh_ref, sc_ref, wrt_ref, br_ref, h_ref, r_ref, *, sel):
    i = pl.program_id(0)
    h = _norm_mod(x_ref[...], g_ref[...], sh_ref, sc_ref, i, **sel)
    h_ref[...] = h
    logits = lax.dot_general(wrt_ref[...], h, _NT, precision=HIGHEST, preferred_element_type=F32)
    aff = _sigmoid(logits)
    sel_s = aff + br_ref[...]
    a = [aff[e:e + 1, :] for e in range(N_EXPERTS)]
    s = [sel_s[e:e + 1, :] for e in range(N_EXPERTS)]
    n = EXPERTS_PER_GROUP
    scores = []
    for grp in range(N_GROUPS):
        v = s[grp * n:(grp + 1) * n]
        best = v[0] + v[1]
        for p in range(n):
            for q in range(p + 1, n):
                if (p, q) != (0, 1):
                    best = jnp.maximum(best, v[p] + v[q])
        scores.append(best)
    gi, _ = _first_argmax(scores)
    cv = [_pick(gi, [s[grp * n + j] for grp in range(N_GROUPS)]) for j in range(n)]
    av = [_pick(gi, [a[grp * n + j] for grp in range(N_GROUPS)]) for j in range(n)]
    i1, _ = _first_argmax(cv)
    cv2 = [jnp.where(i1 == float(j), -jnp.inf, cv[j]) for j in range(n)]
    i2, _ = _first_argmax(cv2)
    w1 = _pick(i1, av)
    w2 = _pick(i2, av)
    tot = w1 + w2
    r_ref[0:1, :] = gi * float(n) + i1
    r_ref[1:2, :] = gi * float(n) + i2
    r_ref[2:3, :] = w1 / tot
    r_ref[3:4, :] = w2 / tot
    r_ref[4:8, :] = jnp.zeros((4, r_ref.shape[1]), F32)


def _ffn_pre(xs, g, mods, w_router_t, b_router, *, sel):
    t, d = xs.shape
    tm = sel["tm"]
    mp = mods.shape[0]
    ne = w_router_t.shape[0]
    return pl.pallas_call(
        functools.partial(_ffn_pre_kernel, sel=sel),
        name="ffn_pre_router",
        grid=(t // tm,),
        in_specs=[pl.BlockSpec((tm, d), lambda i: (i, 0)),
                  pl.BlockSpec((1, d), lambda i: (0, 0)),
                  pl.BlockSpec((mp, d), lambda i: (0, 3)),
                  pl.BlockSpec((mp, d), lambda i: (0, 4)),
                  pl.BlockSpec((ne, d), lambda i: (0, 0)),
                  pl.BlockSpec((ne, 1), lambda i: (0, 0))],
        out_specs=[pl.BlockSpec((tm, d), lambda i: (i, 0)),
                   pl.BlockSpec((SUBLANES, tm), lambda i: (0, i))],
        out_shape=[jax.ShapeDtypeStruct((t, d), F32), jax.ShapeDtypeStruct((SUBLANES, t), F32)],
        compiler_params=_cparams(("arbitrary",)),
    )(xs, g.reshape(1, d), mods, mods, w_router_t, b_router.reshape(ne, 1))


def _dispatch_metadata(route, tm):
    t = route.shape[1]
    na = 2 * t
    assert na % tm == 0
    i32 = jnp.int32
    ne = N_EXPERTS
    n_pad = ne * tm
    e = route[0:2].astype(i32).reshape(na)
    expert_ids = jnp.arange(ne, dtype=i32)
    counts = jnp.sum((e[:, None] == expert_ids[None, :]).astype(i32), axis=0)
    pad_end = jnp.cumsum((-counts) % tm)
    pad_id = jnp.arange(n_pad, dtype=i32)
    pad_key = jnp.sum((pad_id[:, None] >= pad_end[None, :]).astype(i32), axis=1)
    a_id = jnp.arange(na, dtype=i32)
    keys = jnp.concatenate([e, pad_key])
    dst = jnp.concatenate([a_id, na + pad_id])
    src = jnp.concatenate([jnp.where(a_id >= t, a_id - t, a_id), jnp.zeros((n_pad,), i32)])
    keys_s, dst_s, src_s = lax.sort((keys, dst, src), num_keys=1, is_stable=True)
    te = keys_s[0::tm]
    last_e = jnp.max(jnp.where(counts > 0, expert_ids, 0))
    te = jnp.where(te < ne, te, last_e)
    n_slots = na + n_pad
    _, slot_of = lax.sort((dst_s, jnp.arange(n_slots, dtype=i32)), num_keys=1)
    zeros = jnp.zeros((tm,), i32)
    te_x = jnp.concatenate([te[:1], te, te[-1:]]).astype(i32)
    src_x = jnp.concatenate([zeros, src_s, zeros]).astype(i32)
    return te_x, src_x, slot_of[:na]


def _row_copy(src_hbm, row, dst, dst_row, sem):
    return pltpu.make_async_copy(src_hbm.at[pl.ds(row, 1), :], dst.at[pl.ds(dst_row, 1), :], sem)


def _for_rows(n, fn, unroll=8):
    def body(r, carry):
        fn(r)
        return carry
    lax.fori_loop(0, n, body, 0, unroll=unroll)


def _moe_kernel(te_ref, src_ref, h_hbm, wg_ref, wu_ref, wd_ref, y_ref, x0, x1, gsem, *, tm, n_tiles):
    del te_ref
    i = pl.program_id(0)
    xbufs = (x0, x1)

    def gather_copy(entry, r, buf, sem):
        return _row_copy(h_hbm, src_ref[entry * tm + r], buf, r, sem)

    @pl.when(i == 0)
    def _():
        _for_rows(tm, lambda r: gather_copy(1, r, x0, gsem.at[0]).start())

    for par in (0, 1):
        @pl.when(jnp.logical_and(jnp.logical_and(i >= 1, i <= n_tiles), (i - 1) % 2 == par))
        def _(cur_x=xbufs[par], nxt_x=xbufs[1 - par], par=par):
            _for_rows(tm, lambda r: gather_copy(0, r, cur_x, gsem.at[par]).wait())
            for r in range(tm):
                gather_copy(i + 1, r, nxt_x, gsem.at[1 - par]).start()
            xb = cur_x[...].astype(BF16)
            g = jnp.dot(xb, wg_ref[...], preferred_element_type=F32)
            u = jnp.dot(xb, wu_ref[...], preferred_element_type=F32)
            act = (g * _sigmoid(g) * u).astype(BF16)
            y_ref[...] = jnp.dot(act, wd_ref[...], preferred_element_type=F32)

    @pl.when(i == n_tiles + 1)
    def _():
        p_dummy = n_tiles % 2
        _for_rows(tm, lambda r: gather_copy(0, r, xbufs[p_dummy], gsem.at[p_dummy]).wait())


def _moe_experts(h2, te_x, src_x, wg, wu, wd):
    t, d = h2.shape
    tm = MOE_TILE
    n_tiles = te_x.shape[0] - 2
    ff = wg.shape[2]
    grid_spec = pltpu.PrefetchScalarGridSpec(
        num_scalar_prefetch=2,
        grid=(n_tiles + 2,),
        in_specs=[pl.BlockSpec(memory_space=pl.ANY),
                  pl.BlockSpec((None, d, ff), lambda i, te, src: (te[i], 0, 0)),
                  pl.BlockSpec((None, d, ff), lambda i, te, src: (te[i], 0, 0)),
                  pl.BlockSpec((None, ff, d), lambda i, te, src: (te[i], 0, 0))],
        out_specs=pl.BlockSpec((tm, d), lambda i, te, src: (jnp.clip(i - 1, 0, n_tiles - 1), 0)),
        scratch_shapes=[pltpu.VMEM((tm, d), F32), pltpu.VMEM((tm, d), F32), pltpu.SemaphoreType.DMA((2,))],
    )
    return pl.pallas_call(
        functools.partial(_moe_kernel, tm=tm, n_tiles=n_tiles),
        name="moe_experts",
        grid_spec=grid_spec,
        out_shape=jax.ShapeDtypeStruct((n_tiles * tm, d), F32),
        compiler_params=_cparams(("arbitrary",)),
    )(te_x, src_x, h2, wg, wu, wd)


def _combine_kernel(slot_ref, y_hbm, w0_ref, w1_ref, x_ref, gt_ref, o_ref, yb0, yb1, sem, *, tm, nt, t_total, sel):
    i = pl.program_id(0)
    bufs = (yb0, yb1)

    def copy(tile, r, k, buf, s):
        return _row_copy(y_hbm, slot_ref[k * t_total + tile * tm + r], buf.at[k], r, s)

    def wait_tile(buf, s):
        _for_rows(tm, lambda r: [copy(0, r, k, buf, s).wait() for k in range(2)], unroll=4)

    @pl.when(i == 0)
    def _():
        _for_rows(tm, lambda r: [copy(0, r, k, yb0, sem.at[0]).start() for k in range(2)], unroll=4)

    for par in (0, 1):
        @pl.when(jnp.logical_and(i < nt, i % 2 == par))
        def _(cur=bufs[par], nxt=bufs[1 - par], par=par):
            wait_tile(cur, sem.at[par])
            for r in range(tm):
                for k in range(2):
                    copy(i + 1, r, k, nxt, sem.at[1 - par]).start()
            mix = w0_ref[...] * cur[0] + w1_ref[...] * cur[1]
            o_ref[...] = x_ref[...] + _select_rows(gt_ref, i, **sel) * mix

    @pl.when(i == nt)
    def _():
        wait_tile(bufs[nt % 2], sem.at[nt % 2])


def _moe_combine(ys, slots, route, xs, mods, *, sel):
    t, d = xs.shape
    tm = sel["tm"]
    nt = t // tm
    mp = mods.shape[0]
    w0 = route[2].reshape(t, 1)
    w1 = route[3].reshape(t, 1)
    slots = jnp.concatenate([slots, jnp.zeros((tm,), jnp.int32)])

    def tile(i, sl):
        return (jnp.minimum(i, nt - 1), 0)

    grid_spec = pltpu.PrefetchScalarGridSpec(
        num_scalar_prefetch=1,
        grid=(nt + 1,),
        in_specs=[pl.BlockSpec(memory_space=pl.ANY),
                  pl.BlockSpec((tm, 1), tile),
                  pl.BlockSpec((tm, 1), tile),
                  pl.BlockSpec((tm, d), tile),
                  pl.BlockSpec((mp, d), lambda i, sl: (0, 5))],
        out_specs=pl.BlockSpec((tm, d), tile),
        scratch_shapes=[pltpu.VMEM((2, tm, d), F32), pltpu.VMEM((2, tm, d), F32), pltpu.SemaphoreType.DMA((2,))],
    )
    return pl.pallas_call(
        functools.partial(_combine_kernel, tm=tm, nt=nt, t_total=t, sel=sel),
        name="moe_combine",
        grid_spec=grid_spec,
        out_shape=jax.ShapeDtypeStruct((t, d), F32),
        compiler_params=_cparams(("arbitrary",)),
    )(slots, ys, w0, w1, xs, mods)


def _final_norm_kernel(x_ref, g_ref, o_ref):
    x = x_ref[...]
    o_ref[...] = x * lax.rsqrt(jnp.mean(x * x, axis=-1, keepdims=True) + EPS) * g_ref[...]


def _final_norm(xs, g, *, nb, p_rows, lc):
    t, d = xs.shape
    tm = 256
    tpb = p_rows // tm
    ncc = lc // tm
    lat = tpb - ncc
    return pl.pallas_call(
        _final_norm_kernel,
        name="final_norm",
        grid=(nb, lat),
        in_specs=[pl.BlockSpec((tm, d), lambda b, j: (b * tpb + ncc + j, 0)),
                  pl.BlockSpec((1, d), lambda b, j: (0, 0))],
        out_specs=pl.BlockSpec((tm, d), lambda b, j: (b * lat + j, 0)),
        out_shape=jax.ShapeDtypeStruct((nb * lat * tm, d), F32),
        compiler_params=_cparams(("arbitrary", "arbitrary")),
    )(xs, g.reshape(1, d))


def _rope_tables(n_tokens):
    tpos = np.arange(n_tokens)
    row = (tpos // GRID_W).astype(np.float32)
    col = (tpos % GRID_W).astype(np.float32)
    n_axis = HEAD_DIM // 4
    inv = jnp.asarray(ROPE_BASE, F32) ** (-jnp.arange(n_axis, dtype=F32) / n_axis)
    ang = jnp.concatenate([jnp.asarray(row)[:, None] * inv, jnp.asarray(col)[:, None] * inv], axis=-1)
    cos, sin = jnp.cos(ang), jnp.sin(ang)
    return jnp.concatenate([cos, cos], axis=-1), jnp.concatenate([-sin, sin], axis=-1)


def kernel(x, c, ctx, c_ctx, w_mod, b_mod, g_norm_mix, g_norm_ffn, g_norm_out, w_in_ab, b_gate_ab, g_mlstm, rpb_na, w_out_ab, w_in_cd, g_qnorm, g_knorm, ret_decay_logit, g_ret, w_out_cd, w_router, b_router, w_exp_gate, w_exp_up, w_exp_down):
    nb, s_len, d = x.shape
    lc = ctx.shape[1]
    p_rows = lc + s_len
    heads = d // (2 * HEAD_DIM)
    kvh = heads // Q_PER_KV
    gw = heads * HEAD_DIM
    depth = w_mod.shape[0]
    assert lc % SCAN_CHUNK == 0 and s_len % SCAN_CHUNK == 0 and heads % Q_PER_KV == 0

    xs = jnp.concatenate([ctx, x], axis=1).reshape(nb * p_rows, d)
    mp = -(-(nb + 1) // SUBLANES) * SUBLANES
    cc = jnp.concatenate([c, c_ctx[None, :], jnp.zeros((mp - nb - 1, d), F32)], axis=0)
    mods = _modulation(cc, w_mod, b_mod)

    tm = _row_tile(p_rows)
    sel = dict(lc=lc, tm=tm, tpb=p_rows // tm, nb=nb)
    sel_c = dict(lc=lc, tm=COMBINE_TILE, tpb=p_rows // COMBINE_TILE, nb=nb)
    cosf, sinf = _rope_tables(s_len)
    nc = p_rows // SCAN_CHUNK
    ncp = -(-nc // SUBLANES) * SUBLANES
    seq = dict(nb=nb, heads=heads, p_rows=p_rows, lc=lc, width=gw)
    w_router_t = w_router.T.astype(F32)

    for layer in range(depth):
        ml = mods[layer]
        p = layer // 2
        if layer % 2 == 0:
            w = w_in_ab[p]
            ng = 4 * heads
            w_main = jnp.concatenate([w[:, :4 * gw], w[:, 4 * gw + ng:]], axis=1).astype(BF16)
            w_gate_t = w[:, 4 * gw:4 * gw + ng].T.astype(BF16)
            proj, gates_t = _in_projection(xs, g_norm_mix[layer], ml, w_main, w_gate_t, b_gate_ab[p], sel=sel)
            gates = gates_t.reshape(4, heads, nb, nc, SCAN_CHUNK)
            gates = jnp.pad(gates, ((0, 0), (0, 0), (0, 0), (0, ncp - nc), (0, 0)))
            mix_a = _mlstm(proj, gates, g_mlstm[p], **seq)
            mix_b = _natten(proj, _natten_bias(rpb_na[p], s_len // GRID_W), base=4, **seq)
            w_out = w_out_ab[p]
        else:
            proj = _in_projection(xs, g_norm_mix[layer], ml, w_in_cd[p].astype(BF16), None, None, sel=sel)
            mix_a = _gqa(proj, g_qnorm[p], g_knorm[p], cosf, sinf, **seq)
            mix_b = _retention(proj, ret_decay_logit[p], g_ret[p], cosf, sinf, base=heads + 2 * kvh, **seq)
            w_out = w_out_cd[p]
        xs = _out_projection(mix_a, mix_b, w_out[:gw].astype(BF16), w_out[gw:].astype(BF16), xs, ml,
                             sel=sel, gate_chunk=2)
        h2, route = _ffn_pre(xs, g_norm_ffn[layer], ml, w_router_t, b_router, sel=sel)
        te_x, src_x, slots = _dispatch_metadata(route, MOE_TILE)
        ys = _moe_experts(h2, te_x, src_x, w_exp_gate[layer].astype(BF16),
                          w_exp_up[layer].astype(BF16), w_exp_down[layer].astype(BF16))
        xs = _moe_combine(ys, slots, route, xs, ml, sel=sel_c)
    return _final_norm(xs, g_norm_out, nb=nb, p_rows=p_rows, lc=lc).reshape(nb, s_len, d)
```

```python
import functools

import jax
import jax.numpy as jnp
import numpy as np
from jax import lax
from jax.experimental import pallas as pl
from jax.experimental.pallas import tpu as pltpu

F32 = jnp.float32
BF16 = jnp.bfloat16
HIGHEST = lax.Precision.HIGHEST

HEAD_DIM = 128
GRID_W = 64
NA_WIN_R = 8
NA_WIN_C = 16
ROPE_BASE = 10000.0
N_EXPERTS = 16
N_GROUPS = 4
EXPERTS_PER_GROUP = N_EXPERTS // N_GROUPS
N_MOD = 6
EPS = 1e-6
Q_PER_KV = 4
SCAN_CHUNK = 256
LANES = 128
SUBLANES = 8
MOE_TILE = 256
MAX_COL_TILE = 1536
COMBINE_TILE = 256
NA_Q_ROWS = 4
NA_K_ROWS = NA_Q_ROWS + NA_WIN_R
NEG_BIG = -1e30
VMEM_LIMIT = 56 * 1024 * 1024

_NT = (((1,), (1,)), ((), ()))
_TN = (((0,), (0,)), ((), ()))


def _cparams(sem):
    return pltpu.CompilerParams(dimension_semantics=sem, vmem_limit_bytes=VMEM_LIMIT)


def _sigmoid(x):
    return 1.0 / (1.0 + jnp.exp(-x))


def _log_sigmoid(x):
    return jnp.minimum(x, 0.0) - jnp.log(1.0 + jnp.exp(-jnp.abs(x)))


def _row_tile(p_rows):
    best = 256
    for t in (512, 768):
        if p_rows % t == 0:
            best = t
    return best


def _col_tile(n_cols):
    assert n_cols % LANES == 0
    return max(t for t in range(LANES, MAX_COL_TILE + 1, LANES) if n_cols % t == 0)


def _select_rows(ref, i, *, lc, tm, tpb, nb):
    b = i // tpb
    j = i % tpb
    per_sample = ref[pl.ds(b, 1), :]
    ctx = ref[nb:nb + 1, :]
    pos = j * tm + lax.broadcasted_iota(jnp.int32, (tm, 1), 0)
    return jnp.where(pos < lc, ctx, per_sample)


def _norm_mod(x, g, sh_ref, sc_ref, i, **kw):
    ms = jnp.mean(x * x, axis=-1, keepdims=True)
    y = x * lax.rsqrt(ms + EPS) * g
    return y * (1.0 + _select_rows(sc_ref, i, **kw)) + _select_rows(sh_ref, i, **kw)


def _pack_bf16_pairs(x):
    half = x.shape[1] // 2
    bits = lax.bitcast_convert_type(x.astype(BF16).astype(F32), jnp.uint32)
    return (bits[:, :half] >> 16) | (bits[:, half:] & jnp.uint32(0xFFFF0000))


def _unpack_bf16_pairs(words):
    lo = lax.bitcast_convert_type(words << 16, F32)
    hi = lax.bitcast_convert_type(words & jnp.uint32(0xFFFF0000), F32)
    return jnp.concatenate([lo, hi], axis=1).astype(BF16)


def _rope(x, cosf, sinf):
    return x * cosf + pltpu.roll(x, HEAD_DIM // 2, 1) * sinf


def _mod_kernel(cc_ref, w_ref, b_ref, o_ref):
    a = cc_ref[...]
    a = a * _sigmoid(a)
    o_ref[0] = jnp.dot(a.astype(BF16), w_ref[0].astype(BF16), preferred_element_type=F32) + b_ref[0]


def _modulation(cc, w_mod, b_mod):
    depth, d, n = w_mod.shape
    mp = cc.shape[0]
    tn = 1024
    return pl.pallas_call(
        _mod_kernel,
        name="modulation",
        grid=(depth, n // tn),
        in_specs=[
            pl.BlockSpec((mp, d), lambda l, j: (0, 0)),
            pl.BlockSpec((1, d, tn), lambda l, j: (l, 0, j)),
            pl.BlockSpec((1, 1, tn), lambda l, j: (l, 0, j)),
        ],
        out_specs=pl.BlockSpec((1, mp, tn), lambda l, j: (l, 0, j)),
        out_shape=jax.ShapeDtypeStruct((depth, mp, n), F32),
        compiler_params=_cparams(("arbitrary", "arbitrary")),
    )(cc, w_mod, b_mod.reshape(depth, 1, n))


def _inproj_kernel(*refs, with_gates, sel):
    if with_gates:
        x_ref, g_ref, sh_ref, sc_ref, w_ref, wgt_ref, bg_ref, o_ref, gt_ref, h_scr = refs
    else:
        x_ref, g_ref, sh_ref, sc_ref, w_ref, o_ref, h_scr = refs
    i = pl.program_id(0)

    @pl.when(pl.program_id(1) == 0)
    def _():
        h = _norm_mod(x_ref[...], g_ref[...], sh_ref, sc_ref, i, **sel)
        h_scr[...] = h.astype(BF16)
        if with_gates:
            gt_ref[...] = lax.dot_general(wgt_ref[...], h_scr[...], _NT, preferred_element_type=F32) + bg_ref[...]

    o_ref[...] = jnp.dot(h_scr[...], w_ref[...], preferred_element_type=F32).astype(o_ref.dtype)


def _in_projection(xs, g, mods, w, wgt, bg, *, sel):
    t, d = xs.shape
    n = w.shape[1]
    tn = _col_tile(n)
    tm = sel["tm"]
    mp = mods.shape[0]
    with_gates = wgt is not None
    in_specs = [
        pl.BlockSpec((tm, d), lambda i, j: (i, 0)),
        pl.BlockSpec((1, d), lambda i, j: (0, 0)),
        pl.BlockSpec((mp, d), lambda i, j: (0, 0)),
        pl.BlockSpec((mp, d), lambda i, j: (0, 1)),
        pl.BlockSpec((d, tn), lambda i, j: (0, j)),
    ]
    out_specs = [pl.BlockSpec((tm, tn), lambda i, j: (i, j))]
    out_shape = [jax.ShapeDtypeStruct((t, n), BF16)]
    args = [xs, g.reshape(1, d), mods, mods, w]
    if with_gates:
        ng = wgt.shape[0]
        in_specs += [pl.BlockSpec((ng, d), lambda i, j: (0, 0)), pl.BlockSpec((ng, 1), lambda i, j: (0, 0))]
        out_specs.append(pl.BlockSpec((ng, tm), lambda i, j: (0, i)))
        out_shape.append(jax.ShapeDtypeStruct((ng, t), F32))
        args += [wgt, bg.reshape(ng, 1)]
    res = pl.pallas_call(
        functools.partial(_inproj_kernel, with_gates=with_gates, sel=sel),
        name="in_projection",
        grid=(t // tm, n // tn),
        in_specs=in_specs,
        out_specs=out_specs,
        out_shape=out_shape,
        scratch_shapes=[pltpu.VMEM((tm, d), BF16)],
        compiler_params=_cparams(("arbitrary", "arbitrary")),
    )(*args)
    return res if with_gates else res[0]


def _chunk_order(nc, ncc, reverse):
    ctx = list(range(ncc))
    lat = list(range(ncc, nc))
    return (ctx[::-1] + lat[::-1]) if reverse else (ctx + lat)


def _mlstm_kernel(q_ref, k_ref, v_ref, o_ref, gates_ref, gh_ref, out_ref, vaug, yacc, *, nc, ncc):
    L = SCAN_CHUNK
    d = HEAD_DIM
    scale = d ** -0.5
    p_rows = nc * L
    vaug[:, 0:d] = v_ref[...]
    lane = lax.broadcasted_iota(jnp.int32, (p_rows, d), 1)
    vaug[:, d:2 * d] = jnp.where(lane == 0, 1.0, 0.0).astype(BF16)
    gates = gates_ref[...]
    ii = lax.broadcasted_iota(jnp.int32, (L, L), 0)
    jj = lax.broadcasted_iota(jnp.int32, (L, L), 1)
    eye = (ii == jj).astype(F32)
    lane_row = lax.broadcasted_iota(jnp.int32, (gates.shape[1], L), 1)

    def to_col(rows):
        return lax.dot_general(eye, rows, _NT, precision=HIGHEST, preferred_element_type=F32)

    def running_max(x, reverse):
        k = 1
        while k < L:
            if reverse:
                shifted = jnp.where(lane_row < L - k, pltpu.roll(x, L - k, 1), -jnp.inf)
            else:
                shifted = jnp.where(lane_row >= k, pltpu.roll(x, k, 1), -jnp.inf)
            x = jnp.maximum(x, shifted)
            k *= 2
        return x

    pre = []
    for dirn in (0, 1):
        li = gates[2 * dirn]
        lf = _log_sigmoid(gates[2 * dirn + 1])
        mask = (jj <= ii) if dirn == 0 else (jj >= ii)
        mask_f = mask.astype(F32)
        b_row = lax.dot_general(lf, mask_f, _NT, precision=HIGHEST, preferred_element_type=F32)
        b_col = lax.dot_general(mask_f, lf, _NT, precision=HIGHEST, preferred_element_type=F32)
        a_row = li - b_row
        amax_row = running_max(a_row, dirn == 1)
        last = slice(L - 1, L) if dirn == 0 else slice(0, 1)
        pre.append(dict(mask=mask, a_row=a_row, a_col=to_col(li) - b_col, b_col=b_col, amax_col=to_col(amax_row),
                        b_end=b_row[:, last], amax_end=amax_row[:, last]))
    state = [(jnp.zeros((d, 2 * d), F32), jnp.zeros((1, 1), F32)) for _ in (0, 1)]
    orders = [_chunk_order(nc, ncc, False), _chunk_order(nc, ncc, True)]
    written = set()
    for step in range(nc):
        for dirn in (0, 1):
            c = orders[dirn][step]
            p = pre[dirn]
            c_state, m = state[dirn]
            sl = slice(c * L, (c + 1) * L)
            qc = q_ref[sl, :]
            kc = k_ref[sl, :]
            va = vaug[sl, :]
            b_end = p["b_end"][c:c + 1, :]
            amax_end = p["amax_end"][c:c + 1, :]
            mx = jnp.maximum(m, p["amax_col"][:, c:c + 1])
            w = jnp.exp(jnp.where(p["mask"], p["a_row"][c:c + 1, :] - mx, -jnp.inf)) * scale
            w_inter = jnp.exp(m - mx)
            qk = lax.dot_general(qc, kc, _NT, preferred_element_type=F32)
            s = (qk * w).astype(BF16)
            r = jnp.dot(s, va, preferred_element_type=F32)
            if step > 0:
                r = r + w_inter * jnp.dot(qc, c_state.astype(BF16), preferred_element_type=F32)
            h = r[:, 0:d] / jnp.maximum(jnp.abs(r[:, d:d + 1]), jnp.exp(-(p["b_col"][:, c:c + 1] + mx)))
            if c in written:
                yacc[sl, :] = yacc[sl, :] + h
            else:
                yacc[sl, :] = h
                written.add(c)
            if step == nc - 1:
                continue
            m_end = jnp.maximum(m, amax_end)
            w_end = jnp.exp(p["a_col"][:, c:c + 1] - m_end) * scale
            w_prev = jnp.exp(m - m_end)
            kw = (kc.astype(F32) * w_end).astype(BF16)
            c_state = w_prev * c_state + lax.dot_general(kw, va, _TN, preferred_element_type=F32)
            state[dirn] = (c_state, b_end + m_end)
    y = yacc[...]
    yn = y * lax.rsqrt(jnp.mean(y * y, axis=-1, keepdims=True) + EPS) * gh_ref[...]
    out_ref[...] = (_sigmoid(o_ref[...].astype(F32)) * yn).astype(out_ref.dtype)


def _mlstm(proj, gates, g_head, *, nb, heads, p_rows, lc, width):
    nc = p_rows // SCAN_CHUNK
    ncc = lc // SCAN_CHUNK
    ncp = gates.shape[3]
    d = HEAD_DIM
    t = proj.shape[0]

    def col(base):
        return pl.BlockSpec((p_rows, d), lambda b, h: (b, base * heads + h))

    return pl.pallas_call(
        functools.partial(_mlstm_kernel, nc=nc, ncc=ncc),
        name="mlstm",
        grid=(nb, heads),
        in_specs=[col(0), col(1), col(2), col(3),
                  pl.BlockSpec((4, None, None, ncp, SCAN_CHUNK), lambda b, h: (0, h, b, 0, 0)),
                  pl.BlockSpec((None, 1, d), lambda b, h: (h, 0, 0))],
        out_specs=pl.BlockSpec((p_rows, d), lambda b, h: (b, h)),
        out_shape=jax.ShapeDtypeStruct((t, width), BF16),
        scratch_shapes=[pltpu.VMEM((p_rows, 2 * d), BF16), pltpu.VMEM((p_rows, d), F32)],
        compiler_params=_cparams(("arbitrary", "arbitrary")),
    )(proj, proj, proj, proj, gates, g_head.reshape(heads, 1, d))


def _softmax_pv(parts):
    m = parts[0][0].max(axis=-1, keepdims=True)
    for s, _ in parts[1:]:
        m = jnp.maximum(m, s.max(axis=-1, keepdims=True))
    num = 0.0
    den = 0.0
    for s, v in parts:
        p = jnp.exp(s - m)
        den = den + jnp.sum(p, axis=-1, keepdims=True)
        num = num + jnp.dot(p.astype(BF16), v, preferred_element_type=F32)
    return num / den


def _natten_kernel(q_ref, k_ref, v_ref, bias_ref, out_ref, *, lc, rows):
    scale = HEAD_DIM ** -0.5
    kc = k_ref[0:lc, :]
    vc = v_ref[0:lc, :]
    s_cc = lax.dot_general(q_ref[0:lc, :], kc, _NT, preferred_element_type=F32) * scale
    out_ref[0:lc, :] = _softmax_pv([(s_cc, vc)]).astype(out_ref.dtype)
    n_blocks = rows // NA_Q_ROWS
    for rb in range(n_blocks):
        r = rb * NA_Q_ROWS
        ks = min(max(r - NA_WIN_R // 2, 0), rows - NA_K_ROWS)
        pattern = 0 if rb == 0 else (2 if rb == n_blocks - 1 else 1)
        q_sl = slice(lc + r * GRID_W, lc + (r + NA_Q_ROWS) * GRID_W)
        k_sl = slice(lc + ks * GRID_W, lc + (ks + NA_K_ROWS) * GRID_W)
        qr = q_ref[q_sl, :]
        kb = k_ref[k_sl, :]
        vb = v_ref[k_sl, :]
        s_lat = lax.dot_general(qr, kb, _NT, preferred_element_type=F32) * scale + bias_ref[pattern]
        s_ctx = lax.dot_general(qr, kc, _NT, preferred_element_type=F32) * scale
        out_ref[q_sl, :] = _softmax_pv([(s_lat, vb), (s_ctx, vc)]).astype(out_ref.dtype)


def _natten_bias(rpb, rows):
    assert rows % NA_Q_ROWS == 0 and rows >= NA_K_ROWS
    col = np.arange(GRID_W)
    c0 = np.clip(col - NA_WIN_C // 2, 0, GRID_W - NA_WIN_C)
    col_ok = (col[None, :] >= c0[:, None]) & (col[None, :] < c0[:, None] + NA_WIN_C)
    dc_idx = np.clip(col[None, :] - col[:, None], -(NA_WIN_C - 1), NA_WIN_C - 1) + NA_WIN_C - 1
    bias_c = jnp.where(col_ok[None, None], rpb[:, :, dc_idx], NEG_BIG)
    masked = jnp.full(bias_c[:, 0].shape, NEG_BIG, F32)
    half = NA_WIN_R // 2
    patterns = [(0, lambda a: 0), (-half, lambda a: a), (-NA_WIN_R, lambda a: half)]
    tables = []
    for delta, band0 in patterns:
        q_rows = []
        for a in range(NA_Q_ROWS):
            blocks = []
            for j in range(NA_K_ROWS):
                in_band = band0(a) <= j < band0(a) + NA_WIN_R
                blocks.append(bias_c[:, j + delta - a + NA_WIN_R - 1] if in_band else masked)
            q_rows.append(jnp.concatenate(blocks, axis=-1))
        tables.append(jnp.concatenate(q_rows, axis=-2))
    return jnp.stack(tables, axis=1)


def _natten(proj, bias, *, nb, heads, p_rows, lc, width, base):
    d = HEAD_DIM
    t = proj.shape[0]
    rows = (p_rows - lc) // GRID_W

    def col(k):
        return pl.BlockSpec((p_rows, d), lambda b, h: (b, (base + k) * heads + h))

    return pl.pallas_call(
        functools.partial(_natten_kernel, lc=lc, rows=rows),
        name="natten",
        grid=(nb, heads),
        in_specs=[col(0), col(1), col(2),
                  pl.BlockSpec((None,) + bias.shape[1:], lambda b, h: (h, 0, 0, 0))],
        out_specs=pl.BlockSpec((p_rows, d), lambda b, h: (b, h)),
        out_shape=jax.ShapeDtypeStruct((t, width), BF16),
        compiler_params=_cparams(("arbitrary", "arbitrary")),
    )(proj, proj, proj, bias)


def _head_norm(x, g):
    return x * lax.rsqrt(jnp.mean(x * x, axis=-1, keepdims=True) + EPS) * g


def _gqa_kernel(q_ref, k_ref, v_ref, gq_ref, gk_ref, cos_ref, sin_ref, out_ref, kn_scr, *, lc, tq):
    d = HEAD_DIM
    scale = d ** -0.5
    qt = pl.program_id(2)
    ncc = lc // tq

    @pl.when(qt == 0)
    def _():
        kn = _head_norm(k_ref[...].astype(F32), gk_ref[...])
        kn_scr[0:lc, :] = kn[0:lc].astype(BF16)
        kn_scr[lc:, :] = _rope(kn[lc:], cos_ref[...], sin_ref[...]).astype(BF16)

    def attend(rotate, n_keys):
        for g in range(Q_PER_KV):
            qn = _head_norm(q_ref[:, g * d:(g + 1) * d].astype(F32), gq_ref[...])
            if rotate:
                off = pl.multiple_of((qt - ncc) * tq, tq)
                qn = _rope(qn, cos_ref[pl.ds(off, tq), :], sin_ref[pl.ds(off, tq), :])
            qb = (qn * scale).astype(BF16)
            s = lax.dot_general(qb, kn_scr[0:n_keys, :], _NT, preferred_element_type=F32)
            out_ref[:, g * d:(g + 1) * d] = _softmax_pv([(s, v_ref[0:n_keys, :])]).astype(out_ref.dtype)

    @pl.when(qt < ncc)
    def _():
        attend(False, lc)

    @pl.when(qt >= ncc)
    def _():
        attend(True, kn_scr.shape[0])


def _gqa(proj, g_q, g_k, cosf, sinf, *, nb, heads, p_rows, lc, width):
    d = HEAD_DIM
    kvh = heads // Q_PER_KV
    t = proj.shape[0]
    tq = 256
    nq = p_rows // tq
    s_rows = cosf.shape[0]
    gw = Q_PER_KV * d
    return pl.pallas_call(
        functools.partial(_gqa_kernel, lc=lc, tq=tq),
        name="gqa",
        grid=(nb, kvh, nq),
        in_specs=[pl.BlockSpec((tq, gw), lambda b, kh, i: (b * nq + i, kh)),
                  pl.BlockSpec((p_rows, d), lambda b, kh, i: (b, heads + kh)),
                  pl.BlockSpec((p_rows, d), lambda b, kh, i: (b, heads + kvh + kh)),
                  pl.BlockSpec((1, d), lambda b, kh, i: (0, 0)),
                  pl.BlockSpec((1, d), lambda b, kh, i: (0, 0)),
                  pl.BlockSpec((s_rows, d), lambda b, kh, i: (0, 0)),
                  pl.BlockSpec((s_rows, d), lambda b, kh, i: (0, 0))],
        out_specs=pl.BlockSpec((tq, gw), lambda b, kh, i: (b * nq + i, kh)),
        out_shape=jax.ShapeDtypeStruct((t, width), BF16),
        scratch_shapes=[pltpu.VMEM((p_rows, d), BF16)],
        compiler_params=_cparams(("arbitrary", "arbitrary", "arbitrary")),
    )(proj, proj, proj, g_q.reshape(1, d), g_k.reshape(1, d), cosf, sinf)


def _retention_kernel(q_ref, k_ref, v_ref, g_ref, lg_ref, gh_ref, cos_ref, sin_ref, out_ref,
                      qs, ks, yacc, *, nc, ncc, lc):
    L = SCAN_CHUNK
    d = HEAD_DIM
    scale = d ** -0.5
    cosf = cos_ref[...]
    sinf = sin_ref[...]
    qs[0:lc, :] = q_ref[0:lc, :]
    ks[0:lc, :] = k_ref[0:lc, :]
    qs[lc:, :] = _rope(q_ref[lc:, :].astype(F32), cosf, sinf).astype(BF16)
    ks[lc:, :] = _rope(k_ref[lc:, :].astype(F32), cosf, sinf).astype(BF16)
    ii = lax.broadcasted_iota(jnp.int32, (L, L), 0)
    jj = lax.broadcasted_iota(jnp.int32, (L, L), 1)
    pos = lax.broadcasted_iota(jnp.int32, (L, 1), 0).astype(F32)
    decay_sum = 0.0
    dq, dk, dchunk = [], [], []
    for dirn in (0, 1):
        lg = _log_sigmoid(lg_ref[dirn:dirn + 1, 0:1])
        rel = (ii - jj) if dirn == 0 else (jj - ii)
        decay_sum = decay_sum + jnp.where(rel >= 0, jnp.exp(jnp.maximum(rel, 0).astype(F32) * lg), 0.0)
        if dirn == 0:
            dq.append(jnp.exp((pos + 1.0) * lg))
            dk.append(jnp.exp((L - 1.0 - pos) * lg) * scale)
        else:
            dq.append(jnp.exp((L - pos) * lg))
            dk.append(jnp.exp(pos * lg) * scale)
        dchunk.append(jnp.exp(L * lg))
    decay_sum = decay_sum * scale
    for c in range(nc):
        sl = slice(c * L, (c + 1) * L)
        s = (lax.dot_general(qs[sl, :], ks[sl, :], _NT, preferred_element_type=F32) * decay_sum).astype(BF16)
        yacc[sl, :] = jnp.dot(s, v_ref[sl, :], preferred_element_type=F32)
    state = [jnp.zeros((d, d), F32), jnp.zeros((d, d), F32)]
    orders = [_chunk_order(nc, ncc, False), _chunk_order(nc, ncc, True)]
    for step in range(nc):
        for dirn in (0, 1):
            c = orders[dirn][step]
            sl = slice(c * L, (c + 1) * L)
            kc = ks[sl, :]
            vc = v_ref[sl, :]
            if step > 0:
                inter = jnp.dot(qs[sl, :], state[dirn].astype(BF16), preferred_element_type=F32)
                yacc[sl, :] = yacc[sl, :] + dq[dirn] * inter
            if step < nc - 1:
                kw = (kc.astype(F32) * dk[dirn]).astype(BF16)
                state[dirn] = dchunk[dirn] * state[dirn] + lax.dot_general(kw, vc, _TN, preferred_element_type=F32)
    y = yacc[...]
    yn = y * lax.rsqrt(jnp.mean(y * y, axis=-1, keepdims=True) + EPS) * gh_ref[...]
    gate = g_ref[...].astype(F32)
    out_ref[...] = (gate * _sigmoid(gate) * yn).astype(out_ref.dtype)


def _retention(proj, decay_logit, g_head, cosf, sinf, *, nb, heads, p_rows, lc, width, base):
    d = HEAD_DIM
    nc = p_rows // SCAN_CHUNK
    ncc = lc // SCAN_CHUNK
    t = proj.shape[0]
    s_rows = cosf.shape[0]
    lg = jnp.broadcast_to(decay_logit.T[:, :, None], (heads, 2, d)).astype(F32)
    lg = jnp.concatenate([lg, jnp.zeros((heads, SUBLANES - 2, d), F32)], axis=1)

    def col(k):
        return pl.BlockSpec((p_rows, d), lambda b, h: (b, base + k * heads + h))

    return pl.pallas_call(
        functools.partial(_retention_kernel, nc=nc, ncc=ncc, lc=lc),
        name="retention",
        grid=(nb, heads),
        in_specs=[col(0), col(1), col(2), col(3),
                  pl.BlockSpec((None, SUBLANES, d), lambda b, h: (h, 0, 0)),
                  pl.BlockSpec((None, 1, d), lambda b, h: (h, 0, 0)),
                  pl.BlockSpec((s_rows, d), lambda b, h: (0, 0)),
                  pl.BlockSpec((s_rows, d), lambda b, h: (0, 0))],
        out_specs=pl.BlockSpec((p_rows, d), lambda b, h: (b, h)),
        out_shape=jax.ShapeDtypeStruct((t, width), BF16),
        scratch_shapes=[pltpu.VMEM((p_rows, d), BF16), pltpu.VMEM((p_rows, d), BF16),
                        pltpu.VMEM((p_rows, d), F32)],
        compiler_params=_cparams(("arbitrary", "arbitrary")),
    )(proj, proj, proj, proj, lg, g_head.reshape(heads, 1, d), cosf, sinf)


def _outproj_kernel(a_ref, b_ref, wa_ref, wb_ref, x_ref, gt_ref, o_ref, *, sel):
    i = pl.program_id(0)
    y = (jnp.dot(a_ref[...], wa_ref[...], preferred_element_type=F32)
         + jnp.dot(b_ref[...], wb_ref[...], preferred_element_type=F32))
    o_ref[...] = x_ref[...] + _select_rows(gt_ref, i, **sel) * y


def _out_projection(mix_a, mix_b, w_a, w_b, xs, mods, *, sel, gate_chunk):
    t, d = xs.shape
    gw = mix_a.shape[1]
    tm = sel["tm"]
    tn = min(1024, d)
    mp = mods.shape[0]
    npd = d // tn
    return pl.pallas_call(
        functools.partial(_outproj_kernel, sel=sel),
        name="out_projection",
        grid=(t // tm, npd),
        in_specs=[pl.BlockSpec((tm, gw), lambda i, j: (i, 0)),
                  pl.BlockSpec((tm, gw), lambda i, j: (i, 0)),
                  pl.BlockSpec((gw, tn), lambda i, j: (0, j)),
                  pl.BlockSpec((gw, tn), lambda i, j: (0, j)),
                  pl.BlockSpec((tm, tn), lambda i, j: (i, j)),
                  pl.BlockSpec((mp, tn), lambda i, j: (0, gate_chunk * npd + j))],
        out_specs=pl.BlockSpec((tm, tn), lambda i, j: (i, j)),
        out_shape=jax.ShapeDtypeStruct((t, d), F32),
        compiler_params=_cparams(("arbitrary", "arbitrary")),
    )(mix_a, mix_b, w_a, w_b, xs, mods)


def _first_argmax(vals):
    best = vals[0]
    idx = jnp.zeros_like(best)
    for j in range(1, len(vals)):
        upd = vals[j] > best
        idx = jnp.where(upd, float(j), idx)
        best = jnp.where(upd, vals[j], best)
    return idx, best


def _pick(idx, vals):
    out = vals[-1]
    for j in range(len(vals) - 2, -1, -1):
        out = jnp.where(idx == float(j), vals[j], out)
    return out


def _ffn_pre_kernel(x_ref, g_ref, sh_ref, sc_ref, wrt_ref, br_ref, h_ref, r_ref, *, sel):
    i = pl.program_id(0)
    h = _norm_mod(x_ref[...], g_ref[...], sh_ref, sc_ref, i, **sel)
    h_ref[...] = _pack_bf16_pairs(h)
    logits = lax.dot_general(wrt_ref[...], h, _NT, precision=HIGHEST, preferred_element_type=F32)
    aff = _sigmoid(logits)
    sel_s = aff + br_ref[...]
    a = [aff[e:e + 1, :] for e in range(N_EXPERTS)]
    s = [sel_s[e:e + 1, :] for e in range(N_EXPERTS)]
    n = EXPERTS_PER_GROUP
    scores = []
    for grp in range(N_GROUPS):
        v = s[grp * n:(grp + 1) * n]
        best = v[0] + v[1]
        for p in range(n):
            for q in range(p + 1, n):
                if (p, q) != (0, 1):
                    best = jnp.maximum(best, v[p] + v[q])
        scores.append(best)
    gi, _ = _first_argmax(scores)
    cv = [_pick(gi, [s[grp * n + j] for grp in range(N_GROUPS)]) for j in range(n)]
    av = [_pick(gi, [a[grp * n + j] for grp in range(N_GROUPS)]) for j in range(n)]
    i1, _ = _first_argmax(cv)
    cv2 = [jnp.where(i1 == float(j), -jnp.inf, cv[j]) for j in range(n)]
    i2, _ = _first_argmax(cv2)
    w1 = _pick(i1, av)
    w2 = _pick(i2, av)
    tot = w1 + w2
    r_ref[0:1, :] = gi * float(n) + i1
    r_ref[1:2, :] = gi * float(n) + i2
    r_ref[2:3, :] = w1 / tot
    r_ref[3:4, :] = w2 / tot
    r_ref[4:8, :] = jnp.zeros((4, r_ref.shape[1]), F32)


def _ffn_pre(xs, g, mods, w_router_t, b_router, *, sel):
    t, d = xs.shape
    tm = sel["tm"]
    mp = mods.shape[0]
    ne = w_router_t.shape[0]
    return pl.pallas_call(
        functools.partial(_ffn_pre_kernel, sel=sel),
        name="ffn_pre_router",
        grid=(t // tm,),
        in_specs=[pl.BlockSpec((tm, d), lambda i: (i, 0)),
                  pl.BlockSpec((1, d), lambda i: (0, 0)),
                  pl.BlockSpec((mp, d), lambda i: (0, 3)),
                  pl.BlockSpec((mp, d), lambda i: (0, 4)),
                  pl.BlockSpec((ne, d), lambda i: (0, 0)),
                  pl.BlockSpec((ne, 1), lambda i: (0, 0))],
        out_specs=[pl.BlockSpec((tm, d // 2), lambda i: (i, 0)),
                   pl.BlockSpec((SUBLANES, tm), lambda i: (0, i))],
        out_shape=[jax.ShapeDtypeStruct((t, d // 2), jnp.uint32), jax.ShapeDtypeStruct((SUBLANES, t), F32)],
        compiler_params=_cparams(("arbitrary",)),
    )(xs, g.reshape(1, d), mods, mods, w_router_t, b_router.reshape(ne, 1))


def _dispatch_metadata(route, tm):
    t = route.shape[1]
    na = 2 * t
    assert na % tm == 0
    i32 = jnp.int32
    ne = N_EXPERTS
    n_pad = ne * tm
    e = route[0:2].astype(i32).reshape(na)
    expert_ids = jnp.arange(ne, dtype=i32)
    counts = jnp.sum((e[:, None] == expert_ids[None, :]).astype(i32), axis=0)
    pad_end = jnp.cumsum((-counts) % tm)
    pad_id = jnp.arange(n_pad, dtype=i32)
    pad_key = jnp.sum((pad_id[:, None] >= pad_end[None, :]).astype(i32), axis=1)
    a_id = jnp.arange(na, dtype=i32)
    keys = jnp.concatenate([e, pad_key])
    dst = jnp.concatenate([a_id, na + pad_id])
    src = jnp.concatenate([jnp.where(a_id >= t, a_id - t, a_id), jnp.zeros((n_pad,), i32)])
    keys_s, dst_s, src_s = lax.sort((keys, dst, src), num_keys=1, is_stable=True)
    te = keys_s[0::tm]
    last_e = jnp.max(jnp.where(counts > 0, expert_ids, 0))
    te = jnp.where(te < ne, te, last_e)
    n_slots = na + n_pad
    _, slot_of = lax.sort((dst_s, jnp.arange(n_slots, dtype=i32)), num_keys=1)
    zeros = jnp.zeros((tm,), i32)
    te_x = jnp.concatenate([te[:1], te, te[-1:]]).astype(i32)
    src_x = jnp.concatenate([zeros, src_s, zeros]).astype(i32)
    return te_x, src_x, slot_of[:na]


def _row_copy(src_hbm, row, dst, dst_row, sem):
    return pltpu.make_async_copy(src_hbm.at[pl.ds(row, 1), :], dst.at[pl.ds(dst_row, 1), :], sem)


def _for_rows(n, fn, unroll=8):
    def body(r, carry):
        fn(r)
        return carry
    lax.fori_loop(0, n, body, 0, unroll=unroll)


def _moe_kernel(te_ref, src_ref, h_hbm, wg_ref, wu_ref, wd_ref, y_ref, x0, x1, gsem, *, tm, n_tiles):
    del te_ref
    i = pl.program_id(0)
    xbufs = (x0, x1)

    def gather_copy(entry, r, buf, sem):
        return _row_copy(h_hbm, src_ref[entry * tm + r], buf, r, sem)

    @pl.when(i == 0)
    def _():
        _for_rows(tm, lambda r: gather_copy(1, r, x0, gsem.at[0]).start())

    for par in (0, 1):
        @pl.when(jnp.logical_and(jnp.logical_and(i >= 1, i <= n_tiles), (i - 1) % 2 == par))
        def _(cur_x=xbufs[par], nxt_x=xbufs[1 - par], par=par):
            for r in range(tm):
                gather_copy(i + 1, r, nxt_x, gsem.at[1 - par]).start()
            _for_rows(tm, lambda r: gather_copy(0, r, cur_x, gsem.at[par]).wait())
            xb = _unpack_bf16_pairs(cur_x[...])
            g = jnp.dot(xb, wg_ref[...], preferred_element_type=F32)
            u = jnp.dot(xb, wu_ref[...], preferred_element_type=F32)
            act = (g * _sigmoid(g) * u).astype(BF16)
            y_ref[...] = jnp.dot(act, wd_ref[...], preferred_element_type=F32)

    @pl.when(i == n_tiles + 1)
    def _():
        p_dummy = n_tiles % 2
        _for_rows(tm, lambda r: gather_copy(0, r, xbufs[p_dummy], gsem.at[p_dummy]).wait())


def _moe_experts(h2, te_x, src_x, wg, wu, wd):
    t, dh = h2.shape
    d = wg.shape[1]
    tm = MOE_TILE
    n_tiles = te_x.shape[0] - 2
    ff = wg.shape[2]
    grid_spec = pltpu.PrefetchScalarGridSpec(
        num_scalar_prefetch=2,
        grid=(n_tiles + 2,),
        in_specs=[pl.BlockSpec(memory_space=pl.ANY),
                  pl.BlockSpec((None, d, ff), lambda i, te, src: (te[i], 0, 0)),
                  pl.BlockSpec((None, d, ff), lambda i, te, src: (te[i], 0, 0)),
                  pl.BlockSpec((None, ff, d), lambda i, te, src: (te[i], 0, 0))],
        out_specs=pl.BlockSpec((tm, d), lambda i, te, src: (jnp.clip(i - 1, 0, n_tiles - 1), 0)),
        scratch_shapes=[pltpu.VMEM((tm, dh), jnp.uint32), pltpu.VMEM((tm, dh), jnp.uint32),
                        pltpu.SemaphoreType.DMA((2,))],
    )
    return pl.pallas_call(
        functools.partial(_moe_kernel, tm=tm, n_tiles=n_tiles),
        name="moe_experts",
        grid_spec=grid_spec,
        out_shape=jax.ShapeDtypeStruct((n_tiles * tm, d), F32),
        compiler_params=_cparams(("arbitrary",)),
    )(te_x, src_x, h2, wg, wu, wd)


def _combine_kernel(slot_ref, y_hbm, w0_ref, w1_ref, x_ref, gt_ref, o_ref, yb0, yb1, sem, *, tm, nt, t_total, sel):
    i = pl.program_id(0)
    bufs = (yb0, yb1)

    def copy(tile, r, k, buf, s):
        return _row_copy(y_hbm, slot_ref[k * t_total + tile * tm + r], buf.at[k], r, s)

    def wait_tile(buf, s):
        _for_rows(tm, lambda r: [copy(0, r, k, buf, s).wait() for k in range(2)], unroll=4)

    @pl.when(i == 0)
    def _():
        _for_rows(tm, lambda r: [copy(0, r, k, yb0, sem.at[0]).start() for k in range(2)], unroll=4)

    for par in (0, 1):
        @pl.when(jnp.logical_and(i < nt, i % 2 == par))
        def _(cur=bufs[par], nxt=bufs[1 - par], par=par):
            for r in range(tm):
                for k in range(2):
                    copy(i + 1, r, k, nxt, sem.at[1 - par]).start()
            wait_tile(cur, sem.at[par])
            mix = w0_ref[...] * cur[0] + w1_ref[...] * cur[1]
            o_ref[...] = x_ref[...] + _select_rows(gt_ref, i, **sel) * mix

    @pl.when(i == nt)
    def _():
        wait_tile(bufs[nt % 2], sem.at[nt % 2])


def _moe_combine(ys, slots, route, xs, mods, *, sel):
    t, d = xs.shape
    tm = sel["tm"]
    nt = t // tm
    mp = mods.shape[0]
    w0 = route[2].reshape(t, 1)
    w1 = route[3].reshape(t, 1)
    slots = jnp.concatenate([slots, jnp.zeros((tm,), jnp.int32)])

    def tile(i, sl):
        return (jnp.minimum(i, nt - 1), 0)

    grid_spec = pltpu.PrefetchScalarGridSpec(
        num_scalar_prefetch=1,
        grid=(nt + 1,),
        in_specs=[pl.BlockSpec(memory_space=pl.ANY),
                  pl.BlockSpec((tm, 1), tile),
                  pl.BlockSpec((tm, 1), tile),
                  pl.BlockSpec((tm, d), tile),
                  pl.BlockSpec((mp, d), lambda i, sl: (0, 5))],
        out_specs=pl.BlockSpec((tm, d), tile),
        scratch_shapes=[pltpu.VMEM((2, tm, d), F32), pltpu.VMEM((2, tm, d), F32), pltpu.SemaphoreType.DMA((2,))],
    )
    return pl.pallas_call(
        functools.partial(_combine_kernel, tm=tm, nt=nt, t_total=t, sel=sel),
        name="moe_combine",
        grid_spec=grid_spec,
        out_shape=jax.ShapeDtypeStruct((t, d), F32),
        compiler_params=_cparams(("arbitrary",)),
    )(slots, ys, w0, w1, xs, mods)


def _final_norm_kernel(x_ref, g_ref, o_ref):
    x = x_ref[...]
    o_ref[...] = x * lax.rsqrt(jnp.mean(x * x, axis=-1, keepdims=True) + EPS) * g_ref[...]


def _final_norm(xs, g, *, nb, p_rows, lc):
    t, d = xs.shape
    tm = 256
    tpb = p_rows // tm
    ncc = lc // tm
    lat = tpb - ncc
    return pl.pallas_call(
        _final_norm_kernel,
        name="final_norm",
        grid=(nb, lat),
        in_specs=[pl.BlockSpec((tm, d), lambda b, j: (b * tpb + ncc + j, 0)),
                  pl.BlockSpec((1, d), lambda b, j: (0, 0))],
        out_specs=pl.BlockSpec((tm, d), lambda b, j: (b * lat + j, 0)),
        out_shape=jax.ShapeDtypeStruct((nb * lat * tm, d), F32),
        compiler_params=_cparams(("arbitrary", "arbitrary")),
    )(xs, g.reshape(1, d))


def _rope_tables(n_tokens):
    tpos = np.arange(n_tokens)
    row = (tpos // GRID_W).astype(np.float32)
    col = (tpos % GRID_W).astype(np.float32)
    n_axis = HEAD_DIM // 4
    inv = jnp.asarray(ROPE_BASE, F32) ** (-jnp.arange(n_axis, dtype=F32) / n_axis)
    ang = jnp.concatenate([jnp.asarray(row)[:, None] * inv, jnp.asarray(col)[:, None] * inv], axis=-1)
    cos, sin = jnp.cos(ang), jnp.sin(ang)
    return jnp.concatenate([cos, cos], axis=-1), jnp.concatenate([-sin, sin], axis=-1)


def kernel(x, c, ctx, c_ctx, w_mod, b_mod, g_norm_mix, g_norm_ffn, g_norm_out, w_in_ab, b_gate_ab, g_mlstm, rpb_na, w_out_ab, w_in_cd, g_qnorm, g_knorm, ret_decay_logit, g_ret, w_out_cd, w_router, b_router, w_exp_gate, w_exp_up, w_exp_down):
    nb, s_len, d = x.shape
    lc = ctx.shape[1]
    p_rows = lc + s_len
    heads = d // (2 * HEAD_DIM)
    kvh = heads // Q_PER_KV
    gw = heads * HEAD_DIM
    depth = w_mod.shape[0]
    assert lc % SCAN_CHUNK == 0 and s_len % SCAN_CHUNK == 0 and heads % Q_PER_KV == 0

    xs = jnp.concatenate([ctx, x], axis=1).reshape(nb * p_rows, d)
    mp = -(-(nb + 1) // SUBLANES) * SUBLANES
    cc = jnp.concatenate([c, c_ctx[None, :], jnp.zeros((mp - nb - 1, d), F32)], axis=0)
    mods = _modulation(cc, w_mod, b_mod)

    tm = _row_tile(p_rows)
    sel = dict(lc=lc, tm=tm, tpb=p_rows // tm, nb=nb)
    sel_c = dict(lc=lc, tm=COMBINE_TILE, tpb=p_rows // COMBINE_TILE, nb=nb)
    cosf, sinf = _rope_tables(s_len)
    nc = p_rows // SCAN_CHUNK
    ncp = -(-nc // SUBLANES) * SUBLANES
    seq = dict(nb=nb, heads=heads, p_rows=p_rows, lc=lc, width=gw)
    w_router_t = w_router.T.astype(F32)

    for layer in range(depth):
        ml = mods[layer]
        p = layer // 2
        if layer % 2 == 0:
            w = w_in_ab[p]
            ng = 4 * heads
            w_main = jnp.concatenate([w[:, :4 * gw], w[:, 4 * gw + ng:]], axis=1).astype(BF16)
            w_gate_t = w[:, 4 * gw:4 * gw + ng].T.astype(BF16)
            proj, gates_t = _in_projection(xs, g_norm_mix[layer], ml, w_main, w_gate_t, b_gate_ab[p], sel=sel)
            gates = gates_t.reshape(4, heads, nb, nc, SCAN_CHUNK)
            gates = jnp.pad(gates, ((0, 0), (0, 0), (0, 0), (0, ncp - nc), (0, 0)))
            mix_a = _mlstm(proj, gates, g_mlstm[p], **seq)
            mix_b = _natten(proj, _natten_bias(rpb_na[p], s_len // GRID_W), base=4, **seq)
            w_out = w_out_ab[p]
        else:
            proj = _in_projection(xs, g_norm_mix[layer], ml, w_in_cd[p].astype(BF16), None, None, sel=sel)
            mix_a = _gqa(proj, g_qnorm[p], g_knorm[p], cosf, sinf, **seq)
            mix_b = _retention(proj, ret_decay_logit[p], g_ret[p], cosf, sinf, base=heads + 2 * kvh, **seq)
            w_out = w_out_cd[p]
        xs = _out_projection(mix_a, mix_b, w_out[:gw].astype(BF16), w_out[gw:].astype(BF16), xs, ml,
                             sel=sel, gate_chunk=2)
        h2, route = _ffn_pre(xs, g_norm_ffn[layer], ml, w_router_t, b_router, sel=sel)
        te_x, src_x, slots = _dispatch_metadata(route, MOE_TILE)
        ys = _moe_experts(h2, te_x, src_x, w_exp_gate[layer].astype(BF16),
                          w_exp_up[layer].astype(BF16), w_exp_down[layer].astype(BF16))
        xs = _moe_combine(ys, slots, route, xs, ml, sel=sel_c)
    return _final_norm(xs, g_norm_out, nb=nb, p_rows=p_rows, lc=lc).reshape(nb, s_len, d)
```

```python
import functools

import jax
import jax.numpy as jnp
import numpy as np
from jax import lax
from jax.experimental import pallas as pl
from jax.experimental.pallas import tpu as pltpu

F32 = jnp.float32
BF16 = jnp.bfloat16
HIGHEST = lax.Precision.HIGHEST

HEAD_DIM = 128
GRID_W = 64
NA_WIN_R = 8
NA_WIN_C = 16
ROPE_BASE = 10000.0
N_EXPERTS = 16
N_GROUPS = 4
EXPERTS_PER_GROUP = N_EXPERTS // N_GROUPS
N_MOD = 6
EPS = 1e-6
Q_PER_KV = 4
SCAN_CHUNK = 256
LANES = 128
SUBLANES = 8
MOE_TILE = 256
MAX_COL_TILE = 1536
COMBINE_TILE = 256
FUSED_ROW_TILE = 384
NA_Q_ROWS = 4
NA_K_ROWS = NA_Q_ROWS + NA_WIN_R
NEG_BIG = -1e30
VMEM_LIMIT = 56 * 1024 * 1024

_NT = (((1,), (1,)), ((), ()))
_TN = (((0,), (0,)), ((), ()))


def _cparams(sem):
    return pltpu.CompilerParams(dimension_semantics=sem, vmem_limit_bytes=VMEM_LIMIT)


def _sigmoid(x):
    return 1.0 / (1.0 + jnp.exp(-x))


def _log_sigmoid(x):
    return jnp.minimum(x, 0.0) - jnp.log(1.0 + jnp.exp(-jnp.abs(x)))


def _row_tile(p_rows):
    best = 256
    for t in (512, 768):
        if p_rows % t == 0:
            best = t
    return best


def _col_tile(n_cols):
    assert n_cols % LANES == 0
    return max(t for t in range(LANES, MAX_COL_TILE + 1, LANES) if n_cols % t == 0)


def _select_rows(ref, i, *, lc, tm, tpb, nb):
    b = i // tpb
    j = i % tpb
    per_sample = ref[pl.ds(b, 1), :]
    ctx = ref[nb:nb + 1, :]
    pos = j * tm + lax.broadcasted_iota(jnp.int32, (tm, 1), 0)
    return jnp.where(pos < lc, ctx, per_sample)


def _norm_mod(x, g, sh_ref, sc_ref, i, **kw):
    ms = jnp.mean(x * x, axis=-1, keepdims=True)
    y = x * lax.rsqrt(ms + EPS) * g
    return y * (1.0 + _select_rows(sc_ref, i, **kw)) + _select_rows(sh_ref, i, **kw)


def _pack_bf16_pairs(x):
    half = x.shape[1] // 2
    bits = lax.bitcast_convert_type(x.astype(BF16).astype(F32), jnp.uint32)
    return (bits[:, :half] >> 16) | (bits[:, half:] & jnp.uint32(0xFFFF0000))


def _unpack_bf16_pairs(words):
    lo = lax.bitcast_convert_type(words << 16, F32)
    hi = lax.bitcast_convert_type(words & jnp.uint32(0xFFFF0000), F32)
    return jnp.concatenate([lo, hi], axis=1).astype(BF16)


def _rope(x, cosf, sinf):
    return x * cosf + pltpu.roll(x, HEAD_DIM // 2, 1) * sinf


def _mod_kernel(cc_ref, w_ref, b_ref, o_ref):
    a = cc_ref[...]
    a = a * _sigmoid(a)
    o_ref[0] = jnp.dot(a.astype(BF16), w_ref[0].astype(BF16), preferred_element_type=F32) + b_ref[0]


def _modulation(cc, w_mod, b_mod):
    depth, d, n = w_mod.shape
    mp = cc.shape[0]
    tn = 1024
    return pl.pallas_call(
        _mod_kernel,
        name="modulation",
        grid=(depth, n // tn),
        in_specs=[
            pl.BlockSpec((mp, d), lambda l, j: (0, 0)),
            pl.BlockSpec((1, d, tn), lambda l, j: (l, 0, j)),
            pl.BlockSpec((1, 1, tn), lambda l, j: (l, 0, j)),
        ],
        out_specs=pl.BlockSpec((1, mp, tn), lambda l, j: (l, 0, j)),
        out_shape=jax.ShapeDtypeStruct((depth, mp, n), F32),
        compiler_params=_cparams(("arbitrary", "arbitrary")),
    )(cc, w_mod, b_mod.reshape(depth, 1, n))


def _inproj_kernel(*refs, with_gates, sel):
    if with_gates:
        x_ref, g_ref, sh_ref, sc_ref, w_ref, wgt_ref, bg_ref, o_ref, gt_ref, h_scr = refs
    else:
        x_ref, g_ref, sh_ref, sc_ref, w_ref, o_ref, h_scr = refs
    i = pl.program_id(0)

    @pl.when(pl.program_id(1) == 0)
    def _():
        h = _norm_mod(x_ref[...], g_ref[...], sh_ref, sc_ref, i, **sel)
        h_scr[...] = h.astype(BF16)
        if with_gates:
            gt_ref[...] = lax.dot_general(wgt_ref[...], h_scr[...], _NT, preferred_element_type=F32) + bg_ref[...]

    o_ref[...] = jnp.dot(h_scr[...], w_ref[...], preferred_element_type=F32).astype(o_ref.dtype)


def _in_projection(xs, g, mods, w, wgt, bg, *, sel):
    t, d = xs.shape
    n = w.shape[1]
    tn = _col_tile(n)
    tm = sel["tm"]
    mp = mods.shape[0]
    with_gates = wgt is not None
    in_specs = [
        pl.BlockSpec((tm, d), lambda i, j: (i, 0)),
        pl.BlockSpec((1, d), lambda i, j: (0, 0)),
        pl.BlockSpec((mp, d), lambda i, j: (0, 0)),
        pl.BlockSpec((mp, d), lambda i, j: (0, 1)),
        pl.BlockSpec((d, tn), lambda i, j: (0, j)),
    ]
    out_specs = [pl.BlockSpec((tm, tn), lambda i, j: (i, j))]
    out_shape = [jax.ShapeDtypeStruct((t, n), BF16)]
    args = [xs, g.reshape(1, d), mods, mods, w]
    if with_gates:
        ng = wgt.shape[0]
        in_specs += [pl.BlockSpec((ng, d), lambda i, j: (0, 0)), pl.BlockSpec((ng, 1), lambda i, j: (0, 0))]
        out_specs.append(pl.BlockSpec((ng, tm), lambda i, j: (0, i)))
        out_shape.append(jax.ShapeDtypeStruct((ng, t), F32))
        args += [wgt, bg.reshape(ng, 1)]
    res = pl.pallas_call(
        functools.partial(_inproj_kernel, with_gates=with_gates, sel=sel),
        name="in_projection",
        grid=(t // tm, n // tn),
        in_specs=in_specs,
        out_specs=out_specs,
        out_shape=out_shape,
        scratch_shapes=[pltpu.VMEM((tm, d), BF16)],
        compiler_params=_cparams(("arbitrary", "arbitrary")),
    )(*args)
    return res if with_gates else res[0]


def _chunk_order(nc, ncc, reverse):
    ctx = list(range(ncc))
    lat = list(range(ncc, nc))
    return (ctx[::-1] + lat[::-1]) if reverse else (ctx + lat)


def _mlstm_kernel(q_ref, k_ref, v_ref, o_ref, gates_ref, gh_ref, out_ref, vaug, yacc, *, nc, ncc):
    L = SCAN_CHUNK
    d = HEAD_DIM
    scale = d ** -0.5
    p_rows = nc * L
    vaug[:, 0:d] = v_ref[...]
    lane = lax.broadcasted_iota(jnp.int32, (p_rows, d), 1)
    vaug[:, d:2 * d] = jnp.where(lane == 0, 1.0, 0.0).astype(BF16)
    gates = gates_ref[...]
    ii = lax.broadcasted_iota(jnp.int32, (L, L), 0)
    jj = lax.broadcasted_iota(jnp.int32, (L, L), 1)
    eye = (ii == jj).astype(F32)
    lane_row = lax.broadcasted_iota(jnp.int32, (gates.shape[1], L), 1)

    def to_col(rows):
        return lax.dot_general(eye, rows, _NT, precision=HIGHEST, preferred_element_type=F32)

    def running_max(x, reverse):
        k = 1
        while k < L:
            if reverse:
                shifted = jnp.where(lane_row < L - k, pltpu.roll(x, L - k, 1), -jnp.inf)
            else:
                shifted = jnp.where(lane_row >= k, pltpu.roll(x, k, 1), -jnp.inf)
            x = jnp.maximum(x, shifted)
            k *= 2
        return x

    pre = []
    for dirn in (0, 1):
        li = gates[2 * dirn]
        lf = _log_sigmoid(gates[2 * dirn + 1])
        mask = (jj <= ii) if dirn == 0 else (jj >= ii)
        mask_f = mask.astype(F32)
        b_row = lax.dot_general(lf, mask_f, _NT, precision=HIGHEST, preferred_element_type=F32)
        b_col = lax.dot_general(mask_f, lf, _NT, precision=HIGHEST, preferred_element_type=F32)
        a_row = li - b_row
        amax_row = running_max(a_row, dirn == 1)
        last = slice(L - 1, L) if dirn == 0 else slice(0, 1)
        pre.append(dict(mask=mask, a_row=a_row, a_col=to_col(li) - b_col, b_col=b_col, amax_col=to_col(amax_row),
                        b_end=b_row[:, last], amax_end=amax_row[:, last]))
    state = [(jnp.zeros((d, 2 * d), F32), jnp.zeros((1, 1), F32)) for _ in (0, 1)]
    orders = [_chunk_order(nc, ncc, False), _chunk_order(nc, ncc, True)]
    written = set()
    for step in range(nc):
        for dirn in (0, 1):
            c = orders[dirn][step]
            p = pre[dirn]
            c_state, m = state[dirn]
            sl = slice(c * L, (c + 1) * L)
            qc = q_ref[sl, :]
            kc = k_ref[sl, :]
            va = vaug[sl, :]
            b_end = p["b_end"][c:c + 1, :]
            amax_end = p["amax_end"][c:c + 1, :]
            mx = jnp.maximum(m, p["amax_col"][:, c:c + 1])
            w = jnp.exp(jnp.where(p["mask"], p["a_row"][c:c + 1, :] - mx, -jnp.inf)) * scale
            w_inter = jnp.exp(m - mx)
            qk = lax.dot_general(qc, kc, _NT, preferred_element_type=F32)
            s = (qk * w).astype(BF16)
            r = jnp.dot(s, va, preferred_element_type=F32)
            if step > 0:
                r = r + w_inter * jnp.dot(qc, c_state.astype(BF16), preferred_element_type=F32)
            h = r[:, 0:d] / jnp.maximum(jnp.abs(r[:, d:d + 1]), jnp.exp(-(p["b_col"][:, c:c + 1] + mx)))
            if c in written:
                yacc[sl, :] = yacc[sl, :] + h
            else:
                yacc[sl, :] = h
                written.add(c)
            if step == nc - 1:
                continue
            m_end = jnp.maximum(m, amax_end)
            w_end = jnp.exp(p["a_col"][:, c:c + 1] - m_end) * scale
            w_prev = jnp.exp(m - m_end)
            kw = (kc.astype(F32) * w_end).astype(BF16)
            c_state = w_prev * c_state + lax.dot_general(kw, va, _TN, preferred_element_type=F32)
            state[dirn] = (c_state, b_end + m_end)
    y = yacc[...]
    yn = y * lax.rsqrt(jnp.mean(y * y, axis=-1, keepdims=True) + EPS) * gh_ref[...]
    out_ref[...] = (_sigmoid(o_ref[...].astype(F32)) * yn).astype(out_ref.dtype)


def _mlstm(proj, gates, g_head, *, nb, heads, p_rows, lc, width):
    nc = p_rows // SCAN_CHUNK
    ncc = lc // SCAN_CHUNK
    ncp = gates.shape[3]
    d = HEAD_DIM
    t = proj.shape[0]

    def col(base):
        return pl.BlockSpec((p_rows, d), lambda b, h: (b, base * heads + h))

    return pl.pallas_call(
        functools.partial(_mlstm_kernel, nc=nc, ncc=ncc),
        name="mlstm",
        grid=(nb, heads),
        in_specs=[col(0), col(1), col(2), col(3),
                  pl.BlockSpec((4, None, None, ncp, SCAN_CHUNK), lambda b, h: (0, h, b, 0, 0)),
                  pl.BlockSpec((None, 1, d), lambda b, h: (h, 0, 0))],
        out_specs=pl.BlockSpec((p_rows, d), lambda b, h: (b, h)),
        out_shape=jax.ShapeDtypeStruct((t, width), BF16),
        scratch_shapes=[pltpu.VMEM((p_rows, 2 * d), BF16), pltpu.VMEM((p_rows, d), F32)],
        compiler_params=_cparams(("arbitrary", "arbitrary")),
    )(proj, proj, proj, proj, gates, g_head.reshape(heads, 1, d))


def _softmax_pv(parts):
    m = parts[0][0].max(axis=-1, keepdims=True)
    for s, _ in parts[1:]:
        m = jnp.maximum(m, s.max(axis=-1, keepdims=True))
    num = 0.0
    den = 0.0
    for s, v in parts:
        p = jnp.exp(s - m)
        den = den + jnp.sum(p, axis=-1, keepdims=True)
        num = num + jnp.dot(p.astype(BF16), v, preferred_element_type=F32)
    return num / den


def _natten_kernel(q_ref, k_ref, v_ref, bias_ref, out_ref, *, lc, rows):
    scale = HEAD_DIM ** -0.5
    kc = k_ref[0:lc, :]
    vc = v_ref[0:lc, :]
    s_cc = lax.dot_general(q_ref[0:lc, :], kc, _NT, preferred_element_type=F32) * scale
    out_ref[0:lc, :] = _softmax_pv([(s_cc, vc)]).astype(out_ref.dtype)
    n_blocks = rows // NA_Q_ROWS
    for rb in range(n_blocks):
        r = rb * NA_Q_ROWS
        ks = min(max(r - NA_WIN_R // 2, 0), rows - NA_K_ROWS)
        pattern = 0 if rb == 0 else (2 if rb == n_blocks - 1 else 1)
        q_sl = slice(lc + r * GRID_W, lc + (r + NA_Q_ROWS) * GRID_W)
        k_sl = slice(lc + ks * GRID_W, lc + (ks + NA_K_ROWS) * GRID_W)
        qr = q_ref[q_sl, :]
        kb = k_ref[k_sl, :]
        vb = v_ref[k_sl, :]
        s_lat = lax.dot_general(qr, kb, _NT, preferred_element_type=F32) * scale + bias_ref[pattern]
        s_ctx = lax.dot_general(qr, kc, _NT, preferred_element_type=F32) * scale
        out_ref[q_sl, :] = _softmax_pv([(s_lat, vb), (s_ctx, vc)]).astype(out_ref.dtype)


def _natten_bias(rpb, rows):
    assert rows % NA_Q_ROWS == 0 and rows >= NA_K_ROWS
    col = np.arange(GRID_W)
    c0 = np.clip(col - NA_WIN_C // 2, 0, GRID_W - NA_WIN_C)
    col_ok = (col[None, :] >= c0[:, None]) & (col[None, :] < c0[:, None] + NA_WIN_C)
    dc_idx = np.clip(col[None, :] - col[:, None], -(NA_WIN_C - 1), NA_WIN_C - 1) + NA_WIN_C - 1
    bias_c = jnp.where(col_ok[None, None], rpb[:, :, dc_idx], NEG_BIG)
    masked = jnp.full(bias_c[:, 0].shape, NEG_BIG, F32)
    half = NA_WIN_R // 2
    patterns = [(0, lambda a: 0), (-half, lambda a: a), (-NA_WIN_R, lambda a: half)]
    tables = []
    for delta, band0 in patterns:
        q_rows = []
        for a in range(NA_Q_ROWS):
            blocks = []
            for j in range(NA_K_ROWS):
                in_band = band0(a) <= j < band0(a) + NA_WIN_R
                blocks.append(bias_c[:, j + delta - a + NA_WIN_R - 1] if in_band else masked)
            q_rows.append(jnp.concatenate(blocks, axis=-1))
        tables.append(jnp.concatenate(q_rows, axis=-2))
    return jnp.stack(tables, axis=1)


def _natten(proj, bias, *, nb, heads, p_rows, lc, width, base):
    d = HEAD_DIM
    t = proj.shape[0]
    rows = (p_rows - lc) // GRID_W

    def col(k):
        return pl.BlockSpec((p_rows, d), lambda b, h: (b, (base + k) * heads + h))

    return pl.pallas_call(
        functools.partial(_natten_kernel, lc=lc, rows=rows),
        name="natten",
        grid=(nb, heads),
        in_specs=[col(0), col(1), col(2),
                  pl.BlockSpec((None,) + bias.shape[1:], lambda b, h: (h, 0, 0, 0))],
        out_specs=pl.BlockSpec((p_rows, d), lambda b, h: (b, h)),
        out_shape=jax.ShapeDtypeStruct((t, width), BF16),
        compiler_params=_cparams(("arbitrary", "arbitrary")),
    )(proj, proj, proj, bias)


def _head_norm(x, g):
    return x * lax.rsqrt(jnp.mean(x * x, axis=-1, keepdims=True) + EPS) * g


def _gqa_kernel(q_ref, k_ref, v_ref, gq_ref, gk_ref, cos_ref, sin_ref, out_ref, kn_scr, vaug, *, lc, tq):
    d = HEAD_DIM
    scale = d ** -0.5
    qt = pl.program_id(2)
    ncc = lc // tq

    @pl.when(qt == 0)
    def _():
        kn = _head_norm(k_ref[...].astype(F32), gk_ref[...])
        kn_scr[0:lc, :] = kn[0:lc].astype(BF16)
        kn_scr[lc:, :] = _rope(kn[lc:], cos_ref[...], sin_ref[...]).astype(BF16)
        vaug[:, 0:d] = v_ref[...]
        lane = lax.broadcasted_iota(jnp.int32, (vaug.shape[0], d), 1)
        vaug[:, d:2 * d] = jnp.where(lane == 0, 1.0, 0.0).astype(BF16)

    def attend(rotate, n_keys):
        for g in range(Q_PER_KV):
            qn = _head_norm(q_ref[:, g * d:(g + 1) * d].astype(F32), gq_ref[...])
            if rotate:
                off = pl.multiple_of((qt - ncc) * tq, tq)
                qn = _rope(qn, cos_ref[pl.ds(off, tq), :], sin_ref[pl.ds(off, tq), :])
            qb = (qn * scale).astype(BF16)
            s = lax.dot_general(qb, kn_scr[0:n_keys, :], _NT, preferred_element_type=F32)
            p = jnp.exp((s - s.max(axis=-1, keepdims=True)).astype(BF16))
            acc = jnp.dot(p, vaug[0:n_keys, :], preferred_element_type=F32)
            out_ref[:, g * d:(g + 1) * d] = (acc[:, 0:d] / acc[:, d:d + 1]).astype(out_ref.dtype)

    @pl.when(qt < ncc)
    def _():
        attend(False, lc)

    @pl.when(qt >= ncc)
    def _():
        attend(True, kn_scr.shape[0])


def _gqa(proj, g_q, g_k, cosf, sinf, *, nb, heads, p_rows, lc, width):
    d = HEAD_DIM
    kvh = heads // Q_PER_KV
    t = proj.shape[0]
    tq = 256
    nq = p_rows // tq
    s_rows = cosf.shape[0]
    gw = Q_PER_KV * d
    return pl.pallas_call(
        functools.partial(_gqa_kernel, lc=lc, tq=tq),
        name="gqa",
        grid=(nb, kvh, nq),
        in_specs=[pl.BlockSpec((tq, gw), lambda b, kh, i: (b * nq + i, kh)),
                  pl.BlockSpec((p_rows, d), lambda b, kh, i: (b, heads + kh)),
                  pl.BlockSpec((p_rows, d), lambda b, kh, i: (b, heads + kvh + kh)),
                  pl.BlockSpec((1, d), lambda b, kh, i: (0, 0)),
                  pl.BlockSpec((1, d), lambda b, kh, i: (0, 0)),
                  pl.BlockSpec((s_rows, d), lambda b, kh, i: (0, 0)),
                  pl.BlockSpec((s_rows, d), lambda b, kh, i: (0, 0))],
        out_specs=pl.BlockSpec((tq, gw), lambda b, kh, i: (b * nq + i, kh)),
        out_shape=jax.ShapeDtypeStruct((t, width), BF16),
        scratch_shapes=[pltpu.VMEM((p_rows, d), BF16), pltpu.VMEM((p_rows, 2 * d), BF16)],
        compiler_params=_cparams(("arbitrary", "arbitrary", "arbitrary")),
    )(proj, proj, proj, g_q.reshape(1, d), g_k.reshape(1, d), cosf, sinf)


def _retention_kernel(q_ref, k_ref, v_ref, g_ref, lg_ref, gh_ref, cos_ref, sin_ref, out_ref,
                      qs, ks, yacc, *, nc, ncc, lc):
    L = SCAN_CHUNK
    d = HEAD_DIM
    scale = d ** -0.5
    cosf = cos_ref[...]
    sinf = sin_ref[...]
    qs[0:lc, :] = q_ref[0:lc, :]
    ks[0:lc, :] = k_ref[0:lc, :]
    qs[lc:, :] = _rope(q_ref[lc:, :].astype(F32), cosf, sinf).astype(BF16)
    ks[lc:, :] = _rope(k_ref[lc:, :].astype(F32), cosf, sinf).astype(BF16)
    ii = lax.broadcasted_iota(jnp.int32, (L, L), 0)
    jj = lax.broadcasted_iota(jnp.int32, (L, L), 1)
    pos = lax.broadcasted_iota(jnp.int32, (L, 1), 0).astype(F32)
    decay_sum = 0.0
    dq, dk, dchunk = [], [], []
    for dirn in (0, 1):
        lg = _log_sigmoid(lg_ref[dirn:dirn + 1, 0:1])
        rel = (ii - jj) if dirn == 0 else (jj - ii)
        decay_sum = decay_sum + jnp.where(rel >= 0, jnp.exp(jnp.maximum(rel, 0).astype(F32) * lg), 0.0)
        if dirn == 0:
            dq.append(jnp.exp((pos + 1.0) * lg))
            dk.append(jnp.exp((L - 1.0 - pos) * lg) * scale)
        else:
            dq.append(jnp.exp((L - pos) * lg))
            dk.append(jnp.exp(pos * lg) * scale)
        dchunk.append(jnp.exp(L * lg))
    decay_sum = decay_sum * scale
    for c in range(nc):
        sl = slice(c * L, (c + 1) * L)
        s = (lax.dot_general(qs[sl, :], ks[sl, :], _NT, preferred_element_type=F32) * decay_sum).astype(BF16)
        yacc[sl, :] = jnp.dot(s, v_ref[sl, :], preferred_element_type=F32)
    state = [jnp.zeros((d, d), F32), jnp.zeros((d, d), F32)]
    orders = [_chunk_order(nc, ncc, False), _chunk_order(nc, ncc, True)]
    for step in range(nc):
        for dirn in (0, 1):
            c = orders[dirn][step]
            sl = slice(c * L, (c + 1) * L)
            kc = ks[sl, :]
            vc = v_ref[sl, :]
            if step > 0:
                inter = jnp.dot(qs[sl, :], state[dirn].astype(BF16), preferred_element_type=F32)
                yacc[sl, :] = yacc[sl, :] + dq[dirn] * inter
            if step < nc - 1:
                kw = (kc.astype(F32) * dk[dirn]).astype(BF16)
                state[dirn] = dchunk[dirn] * state[dirn] + lax.dot_general(kw, vc, _TN, preferred_element_type=F32)
    y = yacc[...]
    yn = y * lax.rsqrt(jnp.mean(y * y, axis=-1, keepdims=True) + EPS) * gh_ref[...]
    gate = g_ref[...].astype(F32)
    out_ref[...] = (gate * _sigmoid(gate) * yn).astype(out_ref.dtype)


def _retention(proj, decay_logit, g_head, cosf, sinf, *, nb, heads, p_rows, lc, width, base):
    d = HEAD_DIM
    nc = p_rows // SCAN_CHUNK
    ncc = lc // SCAN_CHUNK
    t = proj.shape[0]
    s_rows = cosf.shape[0]
    lg = jnp.broadcast_to(decay_logit.T[:, :, None], (heads, 2, d)).astype(F32)
    lg = jnp.concatenate([lg, jnp.zeros((heads, SUBLANES - 2, d), F32)], axis=1)

    def col(k):
        return pl.BlockSpec((p_rows, d), lambda b, h: (b, base + k * heads + h))

    return pl.pallas_call(
        functools.partial(_retention_kernel, nc=nc, ncc=ncc, lc=lc),
        name="retention",
        grid=(nb, heads),
        in_specs=[col(0), col(1), col(2), col(3),
                  pl.BlockSpec((None, SUBLANES, d), lambda b, h: (h, 0, 0)),
                  pl.BlockSpec((None, 1, d), lambda b, h: (h, 0, 0)),
                  pl.BlockSpec((s_rows, d), lambda b, h: (0, 0)),
                  pl.BlockSpec((s_rows, d), lambda b, h: (0, 0))],
        out_specs=pl.BlockSpec((p_rows, d), lambda b, h: (b, h)),
        out_shape=jax.ShapeDtypeStruct((t, width), BF16),
        scratch_shapes=[pltpu.VMEM((p_rows, d), BF16), pltpu.VMEM((p_rows, d), BF16),
                        pltpu.VMEM((p_rows, d), F32)],
        compiler_params=_cparams(("arbitrary", "arbitrary")),
    )(proj, proj, proj, proj, lg, g_head.reshape(heads, 1, d), cosf, sinf)


def _first_argmax(vals):
    best = vals[0]
    idx = jnp.zeros_like(best)
    for j in range(1, len(vals)):
        upd = vals[j] > best
        idx = jnp.where(upd, float(j), idx)
        best = jnp.where(upd, vals[j], best)
    return idx, best


def _pick(idx, vals):
    out = vals[-1]
    for j in range(len(vals) - 2, -1, -1):
        out = jnp.where(idx == float(j), vals[j], out)
    return out


def _outproj_ffn_kernel(a_ref, b_ref, wa_ref, wb_ref, x_ref, gt_ref, g_ref, sh_ref, sc_ref, wrt_ref, br_ref,
                        o_ref, h_ref, r_ref, *, sel):
    i = pl.program_id(0)
    y = (jnp.dot(a_ref[...], wa_ref[...], preferred_element_type=F32)
         + jnp.dot(b_ref[...], wb_ref[...], preferred_element_type=F32))
    x_new = x_ref[...] + _select_rows(gt_ref, i, **sel) * y
    o_ref[...] = x_new
    h = _norm_mod(x_new, g_ref[...], sh_ref, sc_ref, i, **sel)
    h_ref[...] = _pack_bf16_pairs(h)
    logits = lax.dot_general(wrt_ref[...], h, _NT, precision=HIGHEST, preferred_element_type=F32)
    aff = _sigmoid(logits)
    sel_s = aff + br_ref[...]
    a = [aff[e:e + 1, :] for e in range(N_EXPERTS)]
    s = [sel_s[e:e + 1, :] for e in range(N_EXPERTS)]
    n = EXPERTS_PER_GROUP
    scores = []
    for grp in range(N_GROUPS):
        v = s[grp * n:(grp + 1) * n]
        best = v[0] + v[1]
        for p in range(n):
            for q in range(p + 1, n):
                if (p, q) != (0, 1):
                    best = jnp.maximum(best, v[p] + v[q])
        scores.append(best)
    gi, _ = _first_argmax(scores)
    cv = [_pick(gi, [s[grp * n + j] for grp in range(N_GROUPS)]) for j in range(n)]
    av = [_pick(gi, [a[grp * n + j] for grp in range(N_GROUPS)]) for j in range(n)]
    i1, _ = _first_argmax(cv)
    cv2 = [jnp.where(i1 == float(j), -jnp.inf, cv[j]) for j in range(n)]
    i2, _ = _first_argmax(cv2)
    w1 = _pick(i1, av)
    w2 = _pick(i2, av)
    tot = w1 + w2
    r_ref[0:1, :] = gi * float(n) + i1
    r_ref[1:2, :] = gi * float(n) + i2
    r_ref[2:3, :] = w1 / tot
    r_ref[3:4, :] = w2 / tot
    r_ref[4:8, :] = jnp.zeros((4, r_ref.shape[1]), F32)


def _outproj_ffn(mix_a, mix_b, w_a, w_b, xs, g, mods, w_router_t, b_router, *, sel):
    t, d = xs.shape
    gw = mix_a.shape[1]
    tm = sel["tm"]
    mp = mods.shape[0]
    ne = w_router_t.shape[0]

    def rows(width):
        return pl.BlockSpec((tm, width), lambda i: (i, 0))

    def whole(shape):
        return pl.BlockSpec(shape, lambda i: (0, 0))

    def mod_chunk(k):
        return pl.BlockSpec((mp, d), lambda i: (0, k))

    return pl.pallas_call(
        functools.partial(_outproj_ffn_kernel, sel=sel),
        name="out_projection_ffn_pre",
        grid=(t // tm,),
        in_specs=[rows(gw), rows(gw), whole((gw, d)), whole((gw, d)), rows(d),
                  mod_chunk(2), whole((1, d)), mod_chunk(3), mod_chunk(4), whole((ne, d)), whole((ne, 1))],
        out_specs=[rows(d), rows(d // 2), pl.BlockSpec((SUBLANES, tm), lambda i: (0, i))],
        out_shape=[jax.ShapeDtypeStruct((t, d), F32), jax.ShapeDtypeStruct((t, d // 2), jnp.uint32),
                   jax.ShapeDtypeStruct((SUBLANES, t), F32)],
        compiler_params=_cparams(("arbitrary",)),
    )(mix_a, mix_b, w_a, w_b, xs, mods, g.reshape(1, d), mods, mods, w_router_t, b_router.reshape(ne, 1))


def _dispatch_metadata(route, tm):
    t = route.shape[1]
    na = 2 * t
    assert na % tm == 0
    i32 = jnp.int32
    ne = N_EXPERTS
    n_pad = ne * tm
    e = route[0:2].astype(i32).reshape(na)
    expert_ids = jnp.arange(ne, dtype=i32)
    counts = jnp.sum((e[:, None] == expert_ids[None, :]).astype(i32), axis=0)
    pad_end = jnp.cumsum((-counts) % tm)
    pad_id = jnp.arange(n_pad, dtype=i32)
    pad_key = jnp.sum((pad_id[:, None] >= pad_end[None, :]).astype(i32), axis=1)
    a_id = jnp.arange(na, dtype=i32)
    keys = jnp.concatenate([e, pad_key])
    dst = jnp.concatenate([a_id, na + pad_id])
    src = jnp.concatenate([jnp.where(a_id >= t, a_id - t, a_id), jnp.zeros((n_pad,), i32)])
    keys_s, dst_s, src_s = lax.sort((keys, dst, src), num_keys=1, is_stable=True)
    te = keys_s[0::tm]
    last_e = jnp.max(jnp.where(counts > 0, expert_ids, 0))
    te = jnp.where(te < ne, te, last_e)
    n_slots = na + n_pad
    _, slot_of = lax.sort((dst_s, jnp.arange(n_slots, dtype=i32)), num_keys=1)
    zeros = jnp.zeros((tm,), i32)
    te_x = jnp.concatenate([te[:1], te, te[-1:]]).astype(i32)
    src_x = jnp.concatenate([zeros, src_s, zeros]).astype(i32)
    return te_x, src_x, slot_of[:na]


def _row_copy(src_hbm, row, dst, dst_row, sem):
    return pltpu.make_async_copy(src_hbm.at[pl.ds(row, 1), :], dst.at[pl.ds(dst_row, 1), :], sem)


def _for_rows(n, fn, unroll=8):
    def body(r, carry):
        fn(r)
        return carry
    lax.fori_loop(0, n, body, 0, unroll=unroll)


def _moe_kernel(te_ref, src_ref, h_hbm, wg_ref, wu_ref, wd_ref, y_ref, x0, x1, gsem, *, tm, n_tiles):
    del te_ref
    i = pl.program_id(0)
    xbufs = (x0, x1)

    def gather_copy(entry, r, buf, sem):
        return _row_copy(h_hbm, src_ref[entry * tm + r], buf, r, sem)

    @pl.when(i == 0)
    def _():
        _for_rows(tm, lambda r: gather_copy(1, r, x0, gsem.at[0]).start())

    for par in (0, 1):
        @pl.when(jnp.logical_and(jnp.logical_and(i >= 1, i <= n_tiles), (i - 1) % 2 == par))
        def _(cur_x=xbufs[par], nxt_x=xbufs[1 - par], par=par):
            for r in range(tm):
                gather_copy(i + 1, r, nxt_x, gsem.at[1 - par]).start()
            _for_rows(tm, lambda r: gather_copy(0, r, cur_x, gsem.at[par]).wait())
            xb = _unpack_bf16_pairs(cur_x[...])
            g = jnp.dot(xb, wg_ref[...], preferred_element_type=F32)
            u = jnp.dot(xb, wu_ref[...], preferred_element_type=F32)
            act = (g * _sigmoid(g) * u).astype(BF16)
            y_ref[...] = jnp.dot(act, wd_ref[...], preferred_element_type=F32)

    @pl.when(i == n_tiles + 1)
    def _():
        p_dummy = n_tiles % 2
        _for_rows(tm, lambda r: gather_copy(0, r, xbufs[p_dummy], gsem.at[p_dummy]).wait())


def _moe_experts(h2, te_x, src_x, wg, wu, wd):
    t, dh = h2.shape
    d = wg.shape[1]
    tm = MOE_TILE
    n_tiles = te_x.shape[0] - 2
    ff = wg.shape[2]
    grid_spec = pltpu.PrefetchScalarGridSpec(
        num_scalar_prefetch=2,
        grid=(n_tiles + 2,),
        in_specs=[pl.BlockSpec(memory_space=pl.ANY),
                  pl.BlockSpec((None, d, ff), lambda i, te, src: (te[i], 0, 0)),
                  pl.BlockSpec((None, d, ff), lambda i, te, src: (te[i], 0, 0)),
                  pl.BlockSpec((None, ff, d), lambda i, te, src: (te[i], 0, 0))],
        out_specs=pl.BlockSpec((tm, d), lambda i, te, src: (jnp.clip(i - 1, 0, n_tiles - 1), 0)),
        scratch_shapes=[pltpu.VMEM((tm, dh), jnp.uint32), pltpu.VMEM((tm, dh), jnp.uint32),
                        pltpu.SemaphoreType.DMA((2,))],
    )
    return pl.pallas_call(
        functools.partial(_moe_kernel, tm=tm, n_tiles=n_tiles),
        name="moe_experts",
        grid_spec=grid_spec,
        out_shape=jax.ShapeDtypeStruct((n_tiles * tm, d), F32),
        compiler_params=_cparams(("arbitrary",)),
    )(te_x, src_x, h2, wg, wu, wd)


def _combine_kernel(slot_ref, y_hbm, w0_ref, w1_ref, x_ref, gt_ref, o_ref, yb0, yb1, sem, *, tm, nt, t_total, sel):
    i = pl.program_id(0)
    bufs = (yb0, yb1)

    def copy(tile, r, k, buf, s):
        return _row_copy(y_hbm, slot_ref[k * t_total + tile * tm + r], buf.at[k], r, s)

    def wait_tile(buf, s):
        _for_rows(tm, lambda r: [copy(0, r, k, buf, s).wait() for k in range(2)], unroll=4)

    @pl.when(i == 0)
    def _():
        _for_rows(tm, lambda r: [copy(0, r, k, yb0, sem.at[0]).start() for k in range(2)], unroll=4)

    for par in (0, 1):
        @pl.when(jnp.logical_and(i < nt, i % 2 == par))
        def _(cur=bufs[par], nxt=bufs[1 - par], par=par):
            for r in range(tm):
                for k in range(2):
                    copy(i + 1, r, k, nxt, sem.at[1 - par]).start()
            wait_tile(cur, sem.at[par])
            mix = w0_ref[...] * cur[0] + w1_ref[...] * cur[1]
            o_ref[...] = x_ref[...] + _select_rows(gt_ref, i, **sel) * mix

    @pl.when(i == nt)
    def _():
        wait_tile(bufs[nt % 2], sem.at[nt % 2])


def _moe_combine(ys, slots, route, xs, mods, *, sel):
    t, d = xs.shape
    tm = sel["tm"]
    nt = t // tm
    mp = mods.shape[0]
    w0 = route[2].reshape(t, 1)
    w1 = route[3].reshape(t, 1)
    slots = jnp.concatenate([slots, jnp.zeros((tm,), jnp.int32)])

    def tile(i, sl):
        return (jnp.minimum(i, nt - 1), 0)

    grid_spec = pltpu.PrefetchScalarGridSpec(
        num_scalar_prefetch=1,
        grid=(nt + 1,),
        in_specs=[pl.BlockSpec(memory_space=pl.ANY),
                  pl.BlockSpec((tm, 1), tile),
                  pl.BlockSpec((tm, 1), tile),
                  pl.BlockSpec((tm, d), tile),
                  pl.BlockSpec((mp, d), lambda i, sl: (0, 5))],
        out_specs=pl.BlockSpec((tm, d), tile),
        scratch_shapes=[pltpu.VMEM((2, tm, d), F32), pltpu.VMEM((2, tm, d), F32), pltpu.SemaphoreType.DMA((2,))],
    )
    return pl.pallas_call(
        functools.partial(_combine_kernel, tm=tm, nt=nt, t_total=t, sel=sel),
        name="moe_combine",
        grid_spec=grid_spec,
        out_shape=jax.ShapeDtypeStruct((t, d), F32),
        compiler_params=_cparams(("arbitrary",)),
    )(slots, ys, w0, w1, xs, mods)


def _final_norm_kernel(x_ref, g_ref, o_ref):
    x = x_ref[...]
    o_ref[...] = x * lax.rsqrt(jnp.mean(x * x, axis=-1, keepdims=True) + EPS) * g_ref[...]


def _final_norm(xs, g, *, nb, p_rows, lc):
    t, d = xs.shape
    tm = 256
    tpb = p_rows // tm
    ncc = lc // tm
    lat = tpb - ncc
    return pl.pallas_call(
        _final_norm_kernel,
        name="final_norm",
        grid=(nb, lat),
        in_specs=[pl.BlockSpec((tm, d), lambda b, j: (b * tpb + ncc + j, 0)),
                  pl.BlockSpec((1, d), lambda b, j: (0, 0))],
        out_specs=pl.BlockSpec((tm, d), lambda b, j: (b * lat + j, 0)),
        out_shape=jax.ShapeDtypeStruct((nb * lat * tm, d), F32),
        compiler_params=_cparams(("arbitrary", "arbitrary")),
    )(xs, g.reshape(1, d))


def _rope_tables(n_tokens):
    tpos = np.arange(n_tokens)
    row = (tpos // GRID_W).astype(np.float32)
    col = (tpos % GRID_W).astype(np.float32)
    n_axis = HEAD_DIM // 4
    inv = jnp.asarray(ROPE_BASE, F32) ** (-jnp.arange(n_axis, dtype=F32) / n_axis)
    ang = jnp.concatenate([jnp.asarray(row)[:, None] * inv, jnp.asarray(col)[:, None] * inv], axis=-1)
    cos, sin = jnp.cos(ang), jnp.sin(ang)
    return jnp.concatenate([cos, cos], axis=-1), jnp.concatenate([-sin, sin], axis=-1)


def kernel(x, c, ctx, c_ctx, w_mod, b_mod, g_norm_mix, g_norm_ffn, g_norm_out, w_in_ab, b_gate_ab, g_mlstm, rpb_na, w_out_ab, w_in_cd, g_qnorm, g_knorm, ret_decay_logit, g_ret, w_out_cd, w_router, b_router, w_exp_gate, w_exp_up, w_exp_down):
    nb, s_len, d = x.shape
    lc = ctx.shape[1]
    p_rows = lc + s_len
    heads = d // (2 * HEAD_DIM)
    kvh = heads // Q_PER_KV
    gw = heads * HEAD_DIM
    depth = w_mod.shape[0]
    assert lc % SCAN_CHUNK == 0 and s_len % SCAN_CHUNK == 0 and heads % Q_PER_KV == 0

    xs = jnp.concatenate([ctx, x], axis=1).reshape(nb * p_rows, d)
    mp = -(-(nb + 1) // SUBLANES) * SUBLANES
    cc = jnp.concatenate([c, c_ctx[None, :], jnp.zeros((mp - nb - 1, d), F32)], axis=0)
    mods = _modulation(cc, w_mod, b_mod)

    tm = _row_tile(p_rows)
    sel = dict(lc=lc, tm=tm, tpb=p_rows // tm, nb=nb)
    sel_c = dict(lc=lc, tm=COMBINE_TILE, tpb=p_rows // COMBINE_TILE, nb=nb)
    tm_f = max(t for t in range(LANES, FUSED_ROW_TILE + 1, LANES) if p_rows % t == 0)
    sel_f = dict(lc=lc, tm=tm_f, tpb=p_rows // tm_f, nb=nb)
    cosf, sinf = _rope_tables(s_len)
    nc = p_rows // SCAN_CHUNK
    ncp = -(-nc // SUBLANES) * SUBLANES
    seq = dict(nb=nb, heads=heads, p_rows=p_rows, lc=lc, width=gw)
    w_router_t = w_router.T.astype(F32)

    for layer in range(depth):
        ml = mods[layer]
        p = layer // 2
        if layer % 2 == 0:
            w = w_in_ab[p]
            ng = 4 * heads
            w_main = jnp.concatenate([w[:, :4 * gw], w[:, 4 * gw + ng:]], axis=1).astype(BF16)
            w_gate_t = w[:, 4 * gw:4 * gw + ng].T.astype(BF16)
            proj, gates_t = _in_projection(xs, g_norm_mix[layer], ml, w_main, w_gate_t, b_gate_ab[p], sel=sel)
            gates = gates_t.reshape(4, heads, nb, nc, SCAN_CHUNK)
            gates = jnp.pad(gates, ((0, 0), (0, 0), (0, 0), (0, ncp - nc), (0, 0)))
            mix_a = _mlstm(proj, gates, g_mlstm[p], **seq)
            mix_b = _natten(proj, _natten_bias(rpb_na[p], s_len // GRID_W), base=4, **seq)
            w_out = w_out_ab[p]
        else:
            proj = _in_projection(xs, g_norm_mix[layer], ml, w_in_cd[p].astype(BF16), None, None, sel=sel)
            mix_a = _gqa(proj, g_qnorm[p], g_knorm[p], cosf, sinf, **seq)
            mix_b = _retention(proj, ret_decay_logit[p], g_ret[p], cosf, sinf, base=heads + 2 * kvh, **seq)
            w_out = w_out_cd[p]
        xs, h2, route = _outproj_ffn(mix_a, mix_b, w_out[:gw].astype(BF16), w_out[gw:].astype(BF16), xs,
                                     g_norm_ffn[layer], ml, w_router_t, b_router, sel=sel_f)
        te_x, src_x, slots = _dispatch_metadata(route, MOE_TILE)
        ys = _moe_experts(h2, te_x, src_x, w_exp_gate[layer].astype(BF16),
                          w_exp_up[layer].astype(BF16), w_exp_down[layer].astype(BF16))
        xs = _moe_combine(ys, slots, route, xs, ml, sel=sel_c)
    return _final_norm(xs, g_norm_out, nb=nb, p_rows=p_rows, lc=lc).reshape(nb, s_len, d)
```

```python
import functools

import jax
import jax.numpy as jnp
import numpy as np
from jax import lax
from jax.experimental import pallas as pl
from jax.experimental.pallas import tpu as pltpu

F32 = jnp.float32
BF16 = jnp.bfloat16
HIGHEST = lax.Precision.HIGHEST

HEAD_DIM = 128
GRID_W = 64
NA_WIN_R = 8
NA_WIN_C = 16
ROPE_BASE = 10000.0
N_EXPERTS = 16
N_GROUPS = 4
EXPERTS_PER_GROUP = N_EXPERTS // N_GROUPS
N_MOD = 6
EPS = 1e-6
Q_PER_KV = 4
SCAN_CHUNK = 256
LANES = 128
SUBLANES = 8
MOE_TILE = 256
MAX_COL_TILE = 1536
COMBINE_TILE = 256
NA_Q_ROWS = 4
NA_K_ROWS = NA_Q_ROWS + NA_WIN_R
NEG_BIG = -1e30
VMEM_LIMIT = 56 * 1024 * 1024

_NT = (((1,), (1,)), ((), ()))
_TN = (((0,), (0,)), ((), ()))


def _cparams(sem):
    return pltpu.CompilerParams(dimension_semantics=sem, vmem_limit_bytes=VMEM_LIMIT)


def _sigmoid(x):
    return 1.0 / (1.0 + jnp.exp(-x))


def _log_sigmoid(x):
    return jnp.minimum(x, 0.0) - jnp.log(1.0 + jnp.exp(-jnp.abs(x)))


def _row_tile(p_rows):
    best = 256
    for t in (512, 768):
        if p_rows % t == 0:
            best = t
    return best


def _col_tile(n_cols):
    assert n_cols % LANES == 0
    return max(t for t in range(LANES, MAX_COL_TILE + 1, LANES) if n_cols % t == 0)


def _select_rows(ref, i, *, lc, tm, tpb, nb):
    b = i // tpb
    j = i % tpb
    per_sample = ref[pl.ds(b, 1), :]
    ctx = ref[nb:nb + 1, :]
    pos = j * tm + lax.broadcasted_iota(jnp.int32, (tm, 1), 0)
    return jnp.where(pos < lc, ctx, per_sample)


def _norm_mod(x, g, sh_ref, sc_ref, i, **kw):
    ms = jnp.mean(x * x, axis=-1, keepdims=True)
    y = x * lax.rsqrt(ms + EPS) * g
    return y * (1.0 + _select_rows(sc_ref, i, **kw)) + _select_rows(sh_ref, i, **kw)


def _pack_bf16_pairs(x):
    half = x.shape[1] // 2
    bits = lax.bitcast_convert_type(x.astype(BF16).astype(F32), jnp.uint32)
    return (bits[:, :half] >> 16) | (bits[:, half:] & jnp.uint32(0xFFFF0000))


def _unpack_bf16_pairs(words):
    lo = lax.bitcast_convert_type(words << 16, F32)
    hi = lax.bitcast_convert_type(words & jnp.uint32(0xFFFF0000), F32)
    return jnp.concatenate([lo, hi], axis=1).astype(BF16)


def _rope(x, cosf, sinf):
    return x * cosf + pltpu.roll(x, HEAD_DIM // 2, 1) * sinf


def _mod_kernel(cc_ref, w_ref, b_ref, o_ref):
    a = cc_ref[...]
    a = a * _sigmoid(a)
    o_ref[0] = jnp.dot(a.astype(BF16), w_ref[0].astype(BF16), preferred_element_type=F32) + b_ref[0]


def _modulation(cc, w_mod, b_mod):
    depth, d, n = w_mod.shape
    mp = cc.shape[0]
    tn = 1024
    return pl.pallas_call(
        _mod_kernel,
        name="modulation",
        grid=(depth, n // tn),
        in_specs=[
            pl.BlockSpec((mp, d), lambda l, j: (0, 0)),
            pl.BlockSpec((1, d, tn), lambda l, j: (l, 0, j)),
            pl.BlockSpec((1, 1, tn), lambda l, j: (l, 0, j)),
        ],
        out_specs=pl.BlockSpec((1, mp, tn), lambda l, j: (l, 0, j)),
        out_shape=jax.ShapeDtypeStruct((depth, mp, n), F32),
        compiler_params=_cparams(("arbitrary", "arbitrary")),
    )(cc, w_mod, b_mod.reshape(depth, 1, n))


def _inproj_kernel(*refs, with_gates, sel):
    if with_gates:
        x_ref, g_ref, sh_ref, sc_ref, w_ref, wgt_ref, bg_ref, o_ref, gt_ref, h_scr = refs
    else:
        x_ref, g_ref, sh_ref, sc_ref, w_ref, o_ref, h_scr = refs
    i = pl.program_id(0)

    @pl.when(pl.program_id(1) == 0)
    def _():
        h = _norm_mod(x_ref[...], g_ref[...], sh_ref, sc_ref, i, **sel)
        h_scr[...] = h.astype(BF16)
        if with_gates:
            gt_ref[...] = lax.dot_general(wgt_ref[...], h_scr[...], _NT, preferred_element_type=F32) + bg_ref[...]

    o_ref[...] = jnp.dot(h_scr[...], w_ref[...], preferred_element_type=F32).astype(o_ref.dtype)


def _in_projection(xs, g, mods, w, wgt, bg, *, sel):
    t, d = xs.shape
    n = w.shape[1]
    tn = _col_tile(n)
    tm = sel["tm"]
    mp = mods.shape[0]
    with_gates = wgt is not None
    in_specs = [
        pl.BlockSpec((tm, d), lambda i, j: (i, 0)),
        pl.BlockSpec((1, d), lambda i, j: (0, 0)),
        pl.BlockSpec((mp, d), lambda i, j: (0, 0)),
        pl.BlockSpec((mp, d), lambda i, j: (0, 1)),
        pl.BlockSpec((d, tn), lambda i, j: (0, j)),
    ]
    out_specs = [pl.BlockSpec((tm, tn), lambda i, j: (i, j))]
    out_shape = [jax.ShapeDtypeStruct((t, n), BF16)]
    args = [xs, g.reshape(1, d), mods, mods, w]
    if with_gates:
        ng = wgt.shape[0]
        in_specs += [pl.BlockSpec((ng, d), lambda i, j: (0, 0)), pl.BlockSpec((ng, 1), lambda i, j: (0, 0))]
        out_specs.append(pl.BlockSpec((ng, tm), lambda i, j: (0, i)))
        out_shape.append(jax.ShapeDtypeStruct((ng, t), F32))
        args += [wgt, bg.reshape(ng, 1)]
    res = pl.pallas_call(
        functools.partial(_inproj_kernel, with_gates=with_gates, sel=sel),
        name="in_projection",
        grid=(t // tm, n // tn),
        in_specs=in_specs,
        out_specs=out_specs,
        out_shape=out_shape,
        scratch_shapes=[pltpu.VMEM((tm, d), BF16)],
        compiler_params=_cparams(("arbitrary", "arbitrary")),
    )(*args)
    return res if with_gates else res[0]


def _chunk_order(nc, ncc, reverse):
    ctx = list(range(ncc))
    lat = list(range(ncc, nc))
    return (ctx[::-1] + lat[::-1]) if reverse else (ctx + lat)


def _mlstm_kernel(q_ref, k_ref, v_ref, o_ref, gates_ref, gh_ref, out_ref, vaug, yacc, *, nc, ncc):
    L = SCAN_CHUNK
    d = HEAD_DIM
    scale = d ** -0.5
    p_rows = nc * L
    vaug[:, 0:d] = v_ref[...]
    lane = lax.broadcasted_iota(jnp.int32, (p_rows, d), 1)
    vaug[:, d:2 * d] = jnp.where(lane == 0, 1.0, 0.0).astype(BF16)
    gates = gates_ref[...]
    ii = lax.broadcasted_iota(jnp.int32, (L, L), 0)
    jj = lax.broadcasted_iota(jnp.int32, (L, L), 1)
    eye = (ii == jj).astype(F32)
    lane_row = lax.broadcasted_iota(jnp.int32, (gates.shape[1], L), 1)

    def to_col(rows):
        return lax.dot_general(eye, rows, _NT, precision=HIGHEST, preferred_element_type=F32)

    def running_max(x, reverse):
        k = 1
        while k < L:
            if reverse:
                shifted = jnp.where(lane_row < L - k, pltpu.roll(x, L - k, 1), -jnp.inf)
            else:
                shifted = jnp.where(lane_row >= k, pltpu.roll(x, k, 1), -jnp.inf)
            x = jnp.maximum(x, shifted)
            k *= 2
        return x

    pre = []
    for dirn in (0, 1):
        li = gates[2 * dirn]
        lf = _log_sigmoid(gates[2 * dirn + 1])
        mask = (jj <= ii) if dirn == 0 else (jj >= ii)
        mask_f = mask.astype(F32)
        b_row = lax.dot_general(lf, mask_f, _NT, precision=HIGHEST, preferred_element_type=F32)
        b_col = lax.dot_general(mask_f, lf, _NT, precision=HIGHEST, preferred_element_type=F32)
        a_row = li - b_row
        amax_row = running_max(a_row, dirn == 1)
        last = slice(L - 1, L) if dirn == 0 else slice(0, 1)
        pre.append(dict(mask=mask, a_row=a_row, a_col=to_col(li) - b_col, b_col=b_col, amax_col=to_col(amax_row),
                        b_end=b_row[:, last], amax_end=amax_row[:, last]))
    state = [(jnp.zeros((d, 2 * d), F32), jnp.zeros((1, 1), F32)) for _ in (0, 1)]
    orders = [_chunk_order(nc, ncc, False), _chunk_order(nc, ncc, True)]
    written = set()
    for step in range(nc):
        for dirn in (0, 1):
            c = orders[dirn][step]
            p = pre[dirn]
            c_state, m = state[dirn]
            sl = slice(c * L, (c + 1) * L)
            qc = q_ref[sl, :]
            kc = k_ref[sl, :]
            va = vaug[sl, :]
            b_end = p["b_end"][c:c + 1, :]
            amax_end = p["amax_end"][c:c + 1, :]
            mx = jnp.maximum(m, p["amax_col"][:, c:c + 1])
            w = jnp.exp(jnp.where(p["mask"], p["a_row"][c:c + 1, :] - mx, -jnp.inf)) * scale
            w_inter = jnp.exp(m - mx)
            qk = lax.dot_general(qc, kc, _NT, preferred_element_type=F32)
            s = (qk * w).astype(BF16)
            r = jnp.dot(s, va, preferred_element_type=F32)
            if step > 0:
                r = r + w_inter * jnp.dot(qc, c_state.astype(BF16), preferred_element_type=F32)
            h = r[:, 0:d] / jnp.maximum(jnp.abs(r[:, d:d + 1]), jnp.exp(-(p["b_col"][:, c:c + 1] + mx)))
            if c in written:
                yacc[sl, :] = yacc[sl, :] + h
            else:
                yacc[sl, :] = h
                written.add(c)
            if step == nc - 1:
                continue
            m_end = jnp.maximum(m, amax_end)
            w_end = jnp.exp(p["a_col"][:, c:c + 1] - m_end) * scale
            w_prev = jnp.exp(m - m_end)
            kw = (kc.astype(F32) * w_end).astype(BF16)
            c_state = w_prev * c_state + lax.dot_general(kw, va, _TN, preferred_element_type=F32)
            state[dirn] = (c_state, b_end + m_end)
    y = yacc[...]
    yn = y * lax.rsqrt(jnp.mean(y * y, axis=-1, keepdims=True) + EPS) * gh_ref[...]
    out_ref[...] = (_sigmoid(o_ref[...].astype(F32)) * yn).astype(out_ref.dtype)


def _mlstm(proj, gates, g_head, *, nb, heads, p_rows, lc, width):
    nc = p_rows // SCAN_CHUNK
    ncc = lc // SCAN_CHUNK
    ncp = gates.shape[3]
    d = HEAD_DIM
    t = proj.shape[0]

    def col(base):
        return pl.BlockSpec((p_rows, d), lambda b, h: (b, base * heads + h))

    return pl.pallas_call(
        functools.partial(_mlstm_kernel, nc=nc, ncc=ncc),
        name="mlstm",
        grid=(nb, heads),
        in_specs=[col(0), col(1), col(2), col(3),
                  pl.BlockSpec((4, None, None, ncp, SCAN_CHUNK), lambda b, h: (0, h, b, 0, 0)),
                  pl.BlockSpec((None, 1, d), lambda b, h: (h, 0, 0))],
        out_specs=pl.BlockSpec((p_rows, d), lambda b, h: (b, h)),
        out_shape=jax.ShapeDtypeStruct((t, width), BF16),
        scratch_shapes=[pltpu.VMEM((p_rows, 2 * d), BF16), pltpu.VMEM((p_rows, d), F32)],
        compiler_params=_cparams(("arbitrary", "arbitrary")),
    )(proj, proj, proj, proj, gates, g_head.reshape(heads, 1, d))


def _softmax_pv(parts):
    d = HEAD_DIM
    m = parts[0][0].max(axis=-1, keepdims=True)
    for s, _ in parts[1:]:
        m = jnp.maximum(m, s.max(axis=-1, keepdims=True))
    acc = 0.0
    for s, va in parts:
        acc = acc + jnp.dot(jnp.exp((s - m).astype(BF16)), va, preferred_element_type=F32)
    return acc[:, 0:d] / acc[:, d:d + 1]


def _fill_v_with_ones_column(vaug, v_ref):
    d = HEAD_DIM
    vaug[:, 0:d] = v_ref[...]
    lane = lax.broadcasted_iota(jnp.int32, (vaug.shape[0], d), 1)
    vaug[:, d:2 * d] = jnp.where(lane == 0, 1.0, 0.0).astype(BF16)


def _natten_kernel(q_ref, k_ref, v_in_ref, bias_ref, out_ref, v_ref, *, lc, rows):
    scale = HEAD_DIM ** -0.5
    _fill_v_with_ones_column(v_ref, v_in_ref)
    kc = k_ref[0:lc, :]
    vc = v_ref[0:lc, :]
    s_cc = lax.dot_general(q_ref[0:lc, :], kc, _NT, preferred_element_type=F32) * scale
    out_ref[0:lc, :] = _softmax_pv([(s_cc, vc)]).astype(out_ref.dtype)
    n_blocks = rows // NA_Q_ROWS
    for rb in range(n_blocks):
        r = rb * NA_Q_ROWS
        ks = min(max(r - NA_WIN_R // 2, 0), rows - NA_K_ROWS)
        pattern = 0 if rb == 0 else (2 if rb == n_blocks - 1 else 1)
        q_sl = slice(lc + r * GRID_W, lc + (r + NA_Q_ROWS) * GRID_W)
        k_sl = slice(lc + ks * GRID_W, lc + (ks + NA_K_ROWS) * GRID_W)
        qr = q_ref[q_sl, :]
        kb = k_ref[k_sl, :]
        vb = v_ref[k_sl, :]
        s_lat = lax.dot_general(qr, kb, _NT, preferred_element_type=F32) * scale + bias_ref[pattern]
        s_ctx = lax.dot_general(qr, kc, _NT, preferred_element_type=F32) * scale
        out_ref[q_sl, :] = _softmax_pv([(s_lat, vb), (s_ctx, vc)]).astype(out_ref.dtype)


def _natten_bias(rpb, rows):
    assert rows % NA_Q_ROWS == 0 and rows >= NA_K_ROWS
    col = np.arange(GRID_W)
    c0 = np.clip(col - NA_WIN_C // 2, 0, GRID_W - NA_WIN_C)
    col_ok = (col[None, :] >= c0[:, None]) & (col[None, :] < c0[:, None] + NA_WIN_C)
    dc_idx = np.clip(col[None, :] - col[:, None], -(NA_WIN_C - 1), NA_WIN_C - 1) + NA_WIN_C - 1
    bias_c = jnp.where(col_ok[None, None], rpb[:, :, dc_idx], NEG_BIG)
    masked = jnp.full(bias_c[:, 0].shape, NEG_BIG, F32)
    half = NA_WIN_R // 2
    patterns = [(0, lambda a: 0), (-half, lambda a: a), (-NA_WIN_R, lambda a: half)]
    tables = []
    for delta, band0 in patterns:
        q_rows = []
        for a in range(NA_Q_ROWS):
            blocks = []
            for j in range(NA_K_ROWS):
                in_band = band0(a) <= j < band0(a) + NA_WIN_R
                blocks.append(bias_c[:, j + delta - a + NA_WIN_R - 1] if in_band else masked)
            q_rows.append(jnp.concatenate(blocks, axis=-1))
        tables.append(jnp.concatenate(q_rows, axis=-2))
    return jnp.stack(tables, axis=1)


def _natten(proj, bias, *, nb, heads, p_rows, lc, width, base):
    d = HEAD_DIM
    t = proj.shape[0]
    rows = (p_rows - lc) // GRID_W

    def col(k):
        return pl.BlockSpec((p_rows, d), lambda b, h: (b, (base + k) * heads + h))

    return pl.pallas_call(
        functools.partial(_natten_kernel, lc=lc, rows=rows),
        name="natten",
        grid=(nb, heads),
        in_specs=[col(0), col(1), col(2),
                  pl.BlockSpec((None,) + bias.shape[1:], lambda b, h: (h, 0, 0, 0))],
        out_specs=pl.BlockSpec((p_rows, d), lambda b, h: (b, h)),
        out_shape=jax.ShapeDtypeStruct((t, width), BF16),
        scratch_shapes=[pltpu.VMEM((p_rows, 2 * d), BF16)],
        compiler_params=_cparams(("arbitrary", "arbitrary")),
    )(proj, proj, proj, bias)


def _head_norm(x, g):
    return x * lax.rsqrt(jnp.mean(x * x, axis=-1, keepdims=True) + EPS) * g


def _gqa_kernel(q_ref, k_ref, v_ref, gq_ref, gk_ref, cos_ref, sin_ref, out_ref, kn_scr, vaug, *, lc, tq):
    d = HEAD_DIM
    scale = d ** -0.5
    qt = pl.program_id(2)
    ncc = lc // tq

    @pl.when(qt == 0)
    def _():
        kn = _head_norm(k_ref[...].astype(F32), gk_ref[...])
        kn_scr[0:lc, :] = kn[0:lc].astype(BF16)
        kn_scr[lc:, :] = _rope(kn[lc:], cos_ref[...], sin_ref[...]).astype(BF16)
        _fill_v_with_ones_column(vaug, v_ref)

    def attend(rotate, n_keys):
        for g in range(Q_PER_KV):
            qn = _head_norm(q_ref[:, g * d:(g + 1) * d].astype(F32), gq_ref[...])
            if rotate:
                off = pl.multiple_of((qt - ncc) * tq, tq)
                qn = _rope(qn, cos_ref[pl.ds(off, tq), :], sin_ref[pl.ds(off, tq), :])
            qb = (qn * scale).astype(BF16)
            s = lax.dot_general(qb, kn_scr[0:n_keys, :], _NT, preferred_element_type=F32)
            p = jnp.exp((s - s.max(axis=-1, keepdims=True)).astype(BF16))
            acc = jnp.dot(p, vaug[0:n_keys, :], preferred_element_type=F32)
            out_ref[:, g * d:(g + 1) * d] = (acc[:, 0:d] / acc[:, d:d + 1]).astype(out_ref.dtype)

    @pl.when(qt < ncc)
    def _():
        attend(False, lc)

    @pl.when(qt >= ncc)
    def _():
        attend(True, kn_scr.shape[0])


def _gqa(proj, g_q, g_k, cosf, sinf, *, nb, heads, p_rows, lc, width):
    d = HEAD_DIM
    kvh = heads // Q_PER_KV
    t = proj.shape[0]
    tq = 256
    nq = p_rows // tq
    s_rows = cosf.shape[0]
    gw = Q_PER_KV * d
    return pl.pallas_call(
        functools.partial(_gqa_kernel, lc=lc, tq=tq),
        name="gqa",
        grid=(nb, kvh, nq),
        in_specs=[pl.BlockSpec((tq, gw), lambda b, kh, i: (b * nq + i, kh)),
                  pl.BlockSpec((p_rows, d), lambda b, kh, i: (b, heads + kh)),
                  pl.BlockSpec((p_rows, d), lambda b, kh, i: (b, heads + kvh + kh)),
                  pl.BlockSpec((1, d), lambda b, kh, i: (0, 0)),
                  pl.BlockSpec((1, d), lambda b, kh, i: (0, 0)),
                  pl.BlockSpec((s_rows, d), lambda b, kh, i: (0, 0)),
                  pl.BlockSpec((s_rows, d), lambda b, kh, i: (0, 0))],
        out_specs=pl.BlockSpec((tq, gw), lambda b, kh, i: (b * nq + i, kh)),
        out_shape=jax.ShapeDtypeStruct((t, width), BF16),
        scratch_shapes=[pltpu.VMEM((p_rows, d), BF16), pltpu.VMEM((p_rows, 2 * d), BF16)],
        compiler_params=_cparams(("arbitrary", "arbitrary", "arbitrary")),
    )(proj, proj, proj, g_q.reshape(1, d), g_k.reshape(1, d), cosf, sinf)


def _retention_kernel(q_ref, k_ref, v_ref, g_ref, lg_ref, gh_ref, cos_ref, sin_ref, out_ref,
                      qs, ks, yacc, *, nc, ncc, lc):
    L = SCAN_CHUNK
    d = HEAD_DIM
    scale = d ** -0.5
    cosf = cos_ref[...]
    sinf = sin_ref[...]
    qs[0:lc, :] = q_ref[0:lc, :]
    ks[0:lc, :] = k_ref[0:lc, :]
    qs[lc:, :] = _rope(q_ref[lc:, :].astype(F32), cosf, sinf).astype(BF16)
    ks[lc:, :] = _rope(k_ref[lc:, :].astype(F32), cosf, sinf).astype(BF16)
    ii = lax.broadcasted_iota(jnp.int32, (L, L), 0)
    jj = lax.broadcasted_iota(jnp.int32, (L, L), 1)
    pos = lax.broadcasted_iota(jnp.int32, (L, 1), 0).astype(F32)
    decay_sum = 0.0
    dq, dk, dchunk = [], [], []
    for dirn in (0, 1):
        lg = _log_sigmoid(lg_ref[dirn:dirn + 1, 0:1])
        rel = (ii - jj) if dirn == 0 else (jj - ii)
        decay_sum = decay_sum + jnp.where(rel >= 0, jnp.exp(jnp.maximum(rel, 0).astype(F32) * lg), 0.0)
        if dirn == 0:
            dq.append(jnp.exp((pos + 1.0) * lg))
            dk.append(jnp.exp((L - 1.0 - pos) * lg) * scale)
        else:
            dq.append(jnp.exp((L - pos) * lg))
            dk.append(jnp.exp(pos * lg) * scale)
        dchunk.append(jnp.exp(L * lg))
    decay_sum = decay_sum * scale
    for c in range(nc):
        sl = slice(c * L, (c + 1) * L)
        s = (lax.dot_general(qs[sl, :], ks[sl, :], _NT, preferred_element_type=F32) * decay_sum).astype(BF16)
        yacc[sl, :] = jnp.dot(s, v_ref[sl, :], preferred_element_type=F32)
    state = [jnp.zeros((d, d), F32), jnp.zeros((d, d), F32)]
    orders = [_chunk_order(nc, ncc, False), _chunk_order(nc, ncc, True)]
    for step in range(nc):
        for dirn in (0, 1):
            c = orders[dirn][step]
            sl = slice(c * L, (c + 1) * L)
            kc = ks[sl, :]
            vc = v_ref[sl, :]
            if step > 0:
                inter = jnp.dot(qs[sl, :], state[dirn].astype(BF16), preferred_element_type=F32)
                yacc[sl, :] = yacc[sl, :] + dq[dirn] * inter
            if step < nc - 1:
                kw = (kc.astype(F32) * dk[dirn]).astype(BF16)
                state[dirn] = dchunk[dirn] * state[dirn] + lax.dot_general(kw, vc, _TN, preferred_element_type=F32)
    y = yacc[...]
    yn = y * lax.rsqrt(jnp.mean(y * y, axis=-1, keepdims=True) + EPS) * gh_ref[...]
    gate = g_ref[...].astype(F32)
    out_ref[...] = (gate * _sigmoid(gate) * yn).astype(out_ref.dtype)


def _retention(proj, decay_logit, g_head, cosf, sinf, *, nb, heads, p_rows, lc, width, base):
    d = HEAD_DIM
    nc = p_rows // SCAN_CHUNK
    ncc = lc // SCAN_CHUNK
    t = proj.shape[0]
    s_rows = cosf.shape[0]
    lg = jnp.broadcast_to(decay_logit.T[:, :, None], (heads, 2, d)).astype(F32)
    lg = jnp.concatenate([lg, jnp.zeros((heads, SUBLANES - 2, d), F32)], axis=1)

    def col(k):
        return pl.BlockSpec((p_rows, d), lambda b, h: (b, base + k * heads + h))

    return pl.pallas_call(
        functools.partial(_retention_kernel, nc=nc, ncc=ncc, lc=lc),
        name="retention",
        grid=(nb, heads),
        in_specs=[col(0), col(1), col(2), col(3),
                  pl.BlockSpec((None, SUBLANES, d), lambda b, h: (h, 0, 0)),
                  pl.BlockSpec((None, 1, d), lambda b, h: (h, 0, 0)),
                  pl.BlockSpec((s_rows, d), lambda b, h: (0, 0)),
                  pl.BlockSpec((s_rows, d), lambda b, h: (0, 0))],
        out_specs=pl.BlockSpec((p_rows, d), lambda b, h: (b, h)),
        out_shape=jax.ShapeDtypeStruct((t, width), BF16),
        scratch_shapes=[pltpu.VMEM((p_rows, d), BF16), pltpu.VMEM((p_rows, d), BF16),
                        pltpu.VMEM((p_rows, d), F32)],
        compiler_params=_cparams(("arbitrary", "arbitrary")),
    )(proj, proj, proj, proj, lg, g_head.reshape(heads, 1, d), cosf, sinf)


def _outproj_kernel(a_ref, b_ref, wa_ref, wb_ref, x_ref, gt_ref, o_ref, *, sel):
    i = pl.program_id(0)
    y = (jnp.dot(a_ref[...], wa_ref[...], preferred_element_type=F32)
         + jnp.dot(b_ref[...], wb_ref[...], preferred_element_type=F32))
    o_ref[...] = x_ref[...] + _select_rows(gt_ref, i, **sel) * y


def _out_projection(mix_a, mix_b, w_a, w_b, xs, mods, *, sel, gate_chunk):
    t, d = xs.shape
    gw = mix_a.shape[1]
    tm = sel["tm"]
    tn = min(1024, d)
    mp = mods.shape[0]
    npd = d // tn
    return pl.pallas_call(
        functools.partial(_outproj_kernel, sel=sel),
        name="out_projection",
        grid=(t // tm, npd),
        in_specs=[pl.BlockSpec((tm, gw), lambda i, j: (i, 0)),
                  pl.BlockSpec((tm, gw), lambda i, j: (i, 0)),
                  pl.BlockSpec((gw, tn), lambda i, j: (0, j)),
                  pl.BlockSpec((gw, tn), lambda i, j: (0, j)),
                  pl.BlockSpec((tm, tn), lambda i, j: (i, j)),
                  pl.BlockSpec((mp, tn), lambda i, j: (0, gate_chunk * npd + j))],
        out_specs=pl.BlockSpec((tm, tn), lambda i, j: (i, j)),
        out_shape=jax.ShapeDtypeStruct((t, d), F32),
        compiler_params=_cparams(("arbitrary", "arbitrary")),
    )(mix_a, mix_b, w_a, w_b, xs, mods)


def _first_argmax(vals):
    best = vals[0]
    idx = jnp.zeros_like(best)
    for j in range(1, len(vals)):
        upd = vals[j] > best
        idx = jnp.where(upd, float(j), idx)
        best = jnp.where(upd, vals[j], best)
    return idx, best


def _pick(idx, vals):
    out = vals[-1]
    for j in range(len(vals) - 2, -1, -1):
        out = jnp.where(idx == float(j), vals[j], out)
    return out


def _ffn_pre_kernel(x_ref, g_ref, sh_ref, sc_ref, wrt_ref, br_ref, h_ref, r_ref, *, sel):
    i = pl.program_id(0)
    h = _norm_mod(x_ref[...], g_ref[...], sh_ref, sc_ref, i, **sel)
    h_ref[...] = _pack_bf16_pairs(h)
    logits = lax.dot_general(wrt_ref[...], h, _NT, precision=HIGHEST, preferred_element_type=F32)
    aff = _sigmoid(logits)
    sel_s = aff + br_ref[...]
    a = [aff[e:e + 1, :] for e in range(N_EXPERTS)]
    s = [sel_s[e:e + 1, :] for e in range(N_EXPERTS)]
    n = EXPERTS_PER_GROUP
    scores = []
    for grp in range(N_GROUPS):
        v = s[grp * n:(grp + 1) * n]
        best = v[0] + v[1]
        for p in range(n):
            for q in range(p + 1, n):
                if (p, q) != (0, 1):
                    best = jnp.maximum(best, v[p] + v[q])
        scores.append(best)
    gi, _ = _first_argmax(scores)
    cv = [_pick(gi, [s[grp * n + j] for grp in range(N_GROUPS)]) for j in range(n)]
    av = [_pick(gi, [a[grp * n + j] for grp in range(N_GROUPS)]) for j in range(n)]
    i1, _ = _first_argmax(cv)
    cv2 = [jnp.where(i1 == float(j), -jnp.inf, cv[j]) for j in range(n)]
    i2, _ = _first_argmax(cv2)
    w1 = _pick(i1, av)
    w2 = _pick(i2, av)
    tot = w1 + w2
    r_ref[0:1, :] = gi * float(n) + i1
    r_ref[1:2, :] = gi * float(n) + i2
    r_ref[2:3, :] = w1 / tot
    r_ref[3:4, :] = w2 / tot
    r_ref[4:8, :] = jnp.zeros((4, r_ref.shape[1]), F32)


def _ffn_pre(xs, g, mods, w_router_t, b_router, *, sel):
    t, d = xs.shape
    tm = sel["tm"]
    mp = mods.shape[0]
    ne = w_router_t.shape[0]
    return pl.pallas_call(
        functools.partial(_ffn_pre_kernel, sel=sel),
        name="ffn_pre_router",
        grid=(t // tm,),
        in_specs=[pl.BlockSpec((tm, d), lambda i: (i, 0)),
                  pl.BlockSpec((1, d), lambda i: (0, 0)),
                  pl.BlockSpec((mp, d), lambda i: (0, 3)),
                  pl.BlockSpec((mp, d), lambda i: (0, 4)),
                  pl.BlockSpec((ne, d), lambda i: (0, 0)),
                  pl.BlockSpec((ne, 1), lambda i: (0, 0))],
        out_specs=[pl.BlockSpec((tm, d // 2), lambda i: (i, 0)),
                   pl.BlockSpec((SUBLANES, tm), lambda i: (0, i))],
        out_shape=[jax.ShapeDtypeStruct((t, d // 2), jnp.uint32), jax.ShapeDtypeStruct((SUBLANES, t), F32)],
        compiler_params=_cparams(("arbitrary",)),
    )(xs, g.reshape(1, d), mods, mods, w_router_t, b_router.reshape(ne, 1))


def _dispatch_metadata(route, tm):
    t = route.shape[1]
    na = 2 * t
    assert na % tm == 0
    i32 = jnp.int32
    ne = N_EXPERTS
    n_pad = ne * tm
    e = route[0:2].astype(i32).reshape(na)
    expert_ids = jnp.arange(ne, dtype=i32)
    counts = jnp.sum((e[:, None] == expert_ids[None, :]).astype(i32), axis=0)
    pad_end = jnp.cumsum((-counts) % tm)
    pad_id = jnp.arange(n_pad, dtype=i32)
    pad_key = jnp.sum((pad_id[:, None] >= pad_end[None, :]).astype(i32), axis=1)
    a_id = jnp.arange(na, dtype=i32)
    keys = jnp.concatenate([e, pad_key])
    dst = jnp.concatenate([a_id, na + pad_id])
    src = jnp.concatenate([jnp.where(a_id >= t, a_id - t, a_id), jnp.zeros((n_pad,), i32)])
    keys_s, dst_s, src_s = lax.sort((keys, dst, src), num_keys=1, is_stable=True)
    te = keys_s[0::tm]
    last_e = jnp.max(jnp.where(counts > 0, expert_ids, 0))
    te = jnp.where(te < ne, te, last_e)
    n_slots = na + n_pad
    _, slot_of = lax.sort((dst_s, jnp.arange(n_slots, dtype=i32)), num_keys=1)
    zeros = jnp.zeros((tm,), i32)
    te_x = jnp.concatenate([te[:1], te, te[-1:]]).astype(i32)
    src_x = jnp.concatenate([zeros, src_s, zeros]).astype(i32)
    return te_x, src_x, slot_of[:na]


def _row_copy(src_hbm, row, dst, dst_row, sem):
    return pltpu.make_async_copy(src_hbm.at[pl.ds(row, 1), :], dst.at[pl.ds(dst_row, 1), :], sem)


def _for_rows(n, fn, unroll=8):
    def body(r, carry):
        fn(r)
        return carry
    lax.fori_loop(0, n, body, 0, unroll=unroll)


def _moe_kernel(te_ref, src_ref, h_hbm, wg_ref, wu_ref, wd_ref, y_ref, x0, x1, gsem, *, tm, n_tiles):
    del te_ref
    i = pl.program_id(0)
    xbufs = (x0, x1)

    def gather_copy(entry, r, buf, sem):
        return _row_copy(h_hbm, src_ref[entry * tm + r], buf, r, sem)

    @pl.when(i == 0)
    def _():
        _for_rows(tm, lambda r: gather_copy(1, r, x0, gsem.at[0]).start())

    for par in (0, 1):
        @pl.when(jnp.logical_and(jnp.logical_and(i >= 1, i <= n_tiles), (i - 1) % 2 == par))
        def _(cur_x=xbufs[par], nxt_x=xbufs[1 - par], par=par):
            for r in range(tm):
                gather_copy(i + 1, r, nxt_x, gsem.at[1 - par]).start()
            _for_rows(tm, lambda r: gather_copy(0, r, cur_x, gsem.at[par]).wait())
            xb = _unpack_bf16_pairs(cur_x[...])
            g = jnp.dot(xb, wg_ref[...], preferred_element_type=F32)
            u = jnp.dot(xb, wu_ref[...], preferred_element_type=F32)
            act = (g * _sigmoid(g) * u).astype(BF16)
            y_ref[...] = jnp.dot(act, wd_ref[...], preferred_element_type=F32)

    @pl.when(i == n_tiles + 1)
    def _():
        p_dummy = n_tiles % 2
        _for_rows(tm, lambda r: gather_copy(0, r, xbufs[p_dummy], gsem.at[p_dummy]).wait())


def _moe_experts(h2, te_x, src_x, wg, wu, wd):
    t, dh = h2.shape
    d = wg.shape[1]
    tm = MOE_TILE
    n_tiles = te_x.shape[0] - 2
    ff = wg.shape[2]
    grid_spec = pltpu.PrefetchScalarGridSpec(
        num_scalar_prefetch=2,
        grid=(n_tiles + 2,),
        in_specs=[pl.BlockSpec(memory_space=pl.ANY),
                  pl.BlockSpec((None, d, ff), lambda i, te, src: (te[i], 0, 0)),
                  pl.BlockSpec((None, d, ff), lambda i, te, src: (te[i], 0, 0)),
                  pl.BlockSpec((None, ff, d), lambda i, te, src: (te[i], 0, 0))],
        out_specs=pl.BlockSpec((tm, d), lambda i, te, src: (jnp.clip(i - 1, 0, n_tiles - 1), 0)),
        scratch_shapes=[pltpu.VMEM((tm, dh), jnp.uint32), pltpu.VMEM((tm, dh), jnp.uint32),
                        pltpu.SemaphoreType.DMA((2,))],
    )
    return pl.pallas_call(
        functools.partial(_moe_kernel, tm=tm, n_tiles=n_tiles),
        name="moe_experts",
        grid_spec=grid_spec,
        out_shape=jax.ShapeDtypeStruct((n_tiles * tm, d), F32),
        compiler_params=_cparams(("arbitrary",)),
    )(te_x, src_x, h2, wg, wu, wd)


def _combine_kernel(slot_ref, y_hbm, w0_ref, w1_ref, x_ref, gt_ref, o_ref, yb0, yb1, sem, *, tm, nt, t_total, sel):
    i = pl.program_id(0)
    bufs = (yb0, yb1)

    def copy(tile, r, k, buf, s):
        return _row_copy(y_hbm, slot_ref[k * t_total + tile * tm + r], buf.at[k], r, s)

    def wait_tile(buf, s):
        _for_rows(tm, lambda r: [copy(0, r, k, buf, s).wait() for k in range(2)], unroll=4)

    @pl.when(i == 0)
    def _():
        _for_rows(tm, lambda r: [copy(0, r, k, yb0, sem.at[0]).start() for k in range(2)], unroll=4)

    for par in (0, 1):
        @pl.when(jnp.logical_and(i < nt, i % 2 == par))
        def _(cur=bufs[par], nxt=bufs[1 - par], par=par):
            for r in range(tm):
                for k in range(2):
                    copy(i + 1, r, k, nxt, sem.at[1 - par]).start()
            wait_tile(cur, sem.at[par])
            mix = w0_ref[...] * cur[0] + w1_ref[...] * cur[1]
            o_ref[...] = x_ref[...] + _select_rows(gt_ref, i, **sel) * mix

    @pl.when(i == nt)
    def _():
        wait_tile(bufs[nt % 2], sem.at[nt % 2])


def _moe_combine(ys, slots, route, xs, mods, *, sel):
    t, d = xs.shape
    tm = sel["tm"]
    nt = t // tm
    mp = mods.shape[0]
    w0 = route[2].reshape(t, 1)
    w1 = route[3].reshape(t, 1)
    slots = jnp.concatenate([slots, jnp.zeros((tm,), jnp.int32)])

    def tile(i, sl):
        return (jnp.minimum(i, nt - 1), 0)

    grid_spec = pltpu.PrefetchScalarGridSpec(
        num_scalar_prefetch=1,
        grid=(nt + 1,),
        in_specs=[pl.BlockSpec(memory_space=pl.ANY),
                  pl.BlockSpec((tm, 1), tile),
                  pl.BlockSpec((tm, 1), tile),
                  pl.BlockSpec((tm, d), tile),
                  pl.BlockSpec((mp, d), lambda i, sl: (0, 5))],
        out_specs=pl.BlockSpec((tm, d), tile),
        scratch_shapes=[pltpu.VMEM((2, tm, d), F32), pltpu.VMEM((2, tm, d), F32), pltpu.SemaphoreType.DMA((2,))],
    )
    return pl.pallas_call(
        functools.partial(_combine_kernel, tm=tm, nt=nt, t_total=t, sel=sel),
        name="moe_combine",
        grid_spec=grid_spec,
        out_shape=jax.ShapeDtypeStruct((t, d), F32),
        compiler_params=_cparams(("arbitrary",)),
    )(slots, ys, w0, w1, xs, mods)


def _final_norm_kernel(x_ref, g_ref, o_ref):
    x = x_ref[...]
    o_ref[...] = x * lax.rsqrt(jnp.mean(x * x, axis=-1, keepdims=True) + EPS) * g_ref[...]


def _final_norm(xs, g, *, nb, p_rows, lc):
    t, d = xs.shape
    tm = 256
    tpb = p_rows // tm
    ncc = lc // tm
    lat = tpb - ncc
    return pl.pallas_call(
        _final_norm_kernel,
        name="final_norm",
        grid=(nb, lat),
        in_specs=[pl.BlockSpec((tm, d), lambda b, j: (b * tpb + ncc + j, 0)),
                  pl.BlockSpec((1, d), lambda b, j: (0, 0))],
        out_specs=pl.BlockSpec((tm, d), lambda b, j: (b * lat + j, 0)),
        out_shape=jax.ShapeDtypeStruct((nb * lat * tm, d), F32),
        compiler_params=_cparams(("arbitrary", "arbitrary")),
    )(xs, g.reshape(1, d))


def _rope_tables(n_tokens):
    tpos = np.arange(n_tokens)
    row = (tpos // GRID_W).astype(np.float32)
    col = (tpos % GRID_W).astype(np.float32)
    n_axis = HEAD_DIM // 4
    inv = jnp.asarray(ROPE_BASE, F32) ** (-jnp.arange(n_axis, dtype=F32) / n_axis)
    ang = jnp.concatenate([jnp.asarray(row)[:, None] * inv, jnp.asarray(col)[:, None] * inv], axis=-1)
    cos, sin = jnp.cos(ang), jnp.sin(ang)
    return jnp.concatenate([cos, cos], axis=-1), jnp.concatenate([-sin, sin], axis=-1)


def kernel(x, c, ctx, c_ctx, w_mod, b_mod, g_norm_mix, g_norm_ffn, g_norm_out, w_in_ab, b_gate_ab, g_mlstm, rpb_na, w_out_ab, w_in_cd, g_qnorm, g_knorm, ret_decay_logit, g_ret, w_out_cd, w_router, b_router, w_exp_gate, w_exp_up, w_exp_down):
    nb, s_len, d = x.shape
    lc = ctx.shape[1]
    p_rows = lc + s_len
    heads = d // (2 * HEAD_DIM)
    kvh = heads // Q_PER_KV
    gw = heads * HEAD_DIM
    depth = w_mod.shape[0]
    assert lc % SCAN_CHUNK == 0 and s_len % SCAN_CHUNK == 0 and heads % Q_PER_KV == 0

    xs = jnp.concatenate([ctx, x], axis=1).reshape(nb * p_rows, d)
    mp = -(-(nb + 1) // SUBLANES) * SUBLANES
    cc = jnp.concatenate([c, c_ctx[None, :], jnp.zeros((mp - nb - 1, d), F32)], axis=0)
    mods = _modulation(cc, w_mod, b_mod)

    tm = _row_tile(p_rows)
    sel = dict(lc=lc, tm=tm, tpb=p_rows // tm, nb=nb)
    sel_c = dict(lc=lc, tm=COMBINE_TILE, tpb=p_rows // COMBINE_TILE, nb=nb)
    cosf, sinf = _rope_tables(s_len)
    nc = p_rows // SCAN_CHUNK
    ncp = -(-nc // SUBLANES) * SUBLANES
    seq = dict(nb=nb, heads=heads, p_rows=p_rows, lc=lc, width=gw)
    w_router_t = w_router.T.astype(F32)

    for layer in range(depth):
        ml = mods[layer]
        p = layer // 2
        if layer % 2 == 0:
            w = w_in_ab[p]
            ng = 4 * heads
            w_main = jnp.concatenate([w[:, :4 * gw], w[:, 4 * gw + ng:]], axis=1).astype(BF16)
            w_gate_t = w[:, 4 * gw:4 * gw + ng].T.astype(BF16)
            proj, gates_t = _in_projection(xs, g_norm_mix[layer], ml, w_main, w_gate_t, b_gate_ab[p], sel=sel)
            gates = gates_t.reshape(4, heads, nb, nc, SCAN_CHUNK)
            gates = jnp.pad(gates, ((0, 0), (0, 0), (0, 0), (0, ncp - nc), (0, 0)))
            mix_a = _mlstm(proj, gates, g_mlstm[p], **seq)
            mix_b = _natten(proj, _natten_bias(rpb_na[p], s_len // GRID_W), base=4, **seq)
            w_out = w_out_ab[p]
        else:
            proj = _in_projection(xs, g_norm_mix[layer], ml, w_in_cd[p].astype(BF16), None, None, sel=sel)
            mix_a = _gqa(proj, g_qnorm[p], g_knorm[p], cosf, sinf, **seq)
            mix_b = _retention(proj, ret_decay_logit[p], g_ret[p], cosf, sinf, base=heads + 2 * kvh, **seq)
            w_out = w_out_cd[p]
        xs = _out_projection(mix_a, mix_b, w_out[:gw].astype(BF16), w_out[gw:].astype(BF16), xs, ml,
                             sel=sel, gate_chunk=2)
        h2, route = _ffn_pre(xs, g_norm_ffn[layer], ml, w_router_t, b_router, sel=sel)
        te_x, src_x, slots = _dispatch_metadata(route, MOE_TILE)
        ys = _moe_experts(h2, te_x, src_x, w_exp_gate[layer].astype(BF16),
                          w_exp_up[layer].astype(BF16), w_exp_down[layer].astype(BF16))
        xs = _moe_combine(ys, slots, route, xs, ml, sel=sel_c)
    return _final_norm(xs, g_norm_out, nb=nb, p_rows=p_rows, lc=lc).reshape(nb, s_len, d)
```

```python
import functools

import jax
import jax.numpy as jnp
import numpy as np
from jax import lax
from jax.experimental import pallas as pl
from jax.experimental.pallas import tpu as pltpu

F32 = jnp.float32
BF16 = jnp.bfloat16
HIGHEST = lax.Precision.HIGHEST

HEAD_DIM = 128
GRID_W = 64
NA_WIN_R = 8
NA_WIN_C = 16
ROPE_BASE = 10000.0
N_EXPERTS = 16
N_GROUPS = 4
EXPERTS_PER_GROUP = N_EXPERTS // N_GROUPS
N_MOD = 6
EPS = 1e-6
Q_PER_KV = 4
SCAN_CHUNK = 256
LANES = 128
SUBLANES = 8
MOE_TILE = 256
MAX_COL_TILE = 1536
COMBINE_TILE = 256
NA_Q_ROWS = 4
NA_K_ROWS = NA_Q_ROWS + NA_WIN_R
NEG_BIG = -1e30
VMEM_LIMIT = 56 * 1024 * 1024

_NT = (((1,), (1,)), ((), ()))
_TN = (((0,), (0,)), ((), ()))


def _cparams(sem):
    return pltpu.CompilerParams(dimension_semantics=sem, vmem_limit_bytes=VMEM_LIMIT)


def _sigmoid(x):
    return 1.0 / (1.0 + jnp.exp(-x))


def _log_sigmoid(x):
    return jnp.minimum(x, 0.0) - jnp.log(1.0 + jnp.exp(-jnp.abs(x)))


def _row_tile(p_rows):
    best = 256
    for t in (512, 768):
        if p_rows % t == 0:
            best = t
    return best


def _col_tile(n_cols):
    assert n_cols % LANES == 0
    return max(t for t in range(LANES, MAX_COL_TILE + 1, LANES) if n_cols % t == 0)


def _select_rows(ref, i, *, lc, tm, tpb, nb):
    b = i // tpb
    j = i % tpb
    per_sample = ref[pl.ds(b, 1), :]
    ctx = ref[nb:nb + 1, :]
    pos = j * tm + lax.broadcasted_iota(jnp.int32, (tm, 1), 0)
    return jnp.where(pos < lc, ctx, per_sample)


def _norm_mod(x, g, sh_ref, sc_ref, i, **kw):
    ms = jnp.mean(x * x, axis=-1, keepdims=True)
    y = x * lax.rsqrt(ms + EPS) * g
    return y * (1.0 + _select_rows(sc_ref, i, **kw)) + _select_rows(sh_ref, i, **kw)


def _pack_bf16_pairs(x):
    half = x.shape[1] // 2
    bits = lax.bitcast_convert_type(x.astype(BF16).astype(F32), jnp.uint32)
    return (bits[:, :half] >> 16) | (bits[:, half:] & jnp.uint32(0xFFFF0000))


def _unpack_bf16_pairs(words):
    lo = lax.bitcast_convert_type(words << 16, F32)
    hi = lax.bitcast_convert_type(words & jnp.uint32(0xFFFF0000), F32)
    return jnp.concatenate([lo, hi], axis=1).astype(BF16)


def _rope(x, cosf, sinf):
    return x * cosf + pltpu.roll(x, HEAD_DIM // 2, 1) * sinf


def _mod_kernel(cc_ref, w_ref, b_ref, o_ref):
    a = cc_ref[...]
    a = a * _sigmoid(a)
    o_ref[0] = jnp.dot(a.astype(BF16), w_ref[0].astype(BF16), preferred_element_type=F32) + b_ref[0]


def _modulation(cc, w_mod, b_mod):
    depth, d, n = w_mod.shape
    mp = cc.shape[0]
    tn = 1024
    return pl.pallas_call(
        _mod_kernel,
        name="modulation",
        grid=(depth, n // tn),
        in_specs=[
            pl.BlockSpec((mp, d), lambda l, j: (0, 0)),
            pl.BlockSpec((1, d, tn), lambda l, j: (l, 0, j)),
            pl.BlockSpec((1, 1, tn), lambda l, j: (l, 0, j)),
        ],
        out_specs=pl.BlockSpec((1, mp, tn), lambda l, j: (l, 0, j)),
        out_shape=jax.ShapeDtypeStruct((depth, mp, n), F32),
        compiler_params=_cparams(("arbitrary", "arbitrary")),
    )(cc, w_mod, b_mod.reshape(depth, 1, n))


def _inproj_kernel(*refs, with_gates, sel):
    if with_gates:
        x_ref, g_ref, sh_ref, sc_ref, w_ref, wgt_ref, bg_ref, o_ref, gt_ref, h_scr = refs
    else:
        x_ref, g_ref, sh_ref, sc_ref, w_ref, o_ref, h_scr = refs
    i = pl.program_id(0)

    @pl.when(pl.program_id(1) == 0)
    def _():
        h = _norm_mod(x_ref[...], g_ref[...], sh_ref, sc_ref, i, **sel)
        h_scr[...] = h.astype(BF16)
        if with_gates:
            gt_ref[...] = lax.dot_general(wgt_ref[...], h_scr[...], _NT, preferred_element_type=F32) + bg_ref[...]

    o_ref[...] = jnp.dot(h_scr[...], w_ref[...], preferred_element_type=F32).astype(o_ref.dtype)


def _in_projection(xs, g, mods, w, wgt, bg, *, sel):
    t, d = xs.shape
    n = w.shape[1]
    tn = _col_tile(n)
    tm = sel["tm"]
    mp = mods.shape[0]
    with_gates = wgt is not None
    in_specs = [
        pl.BlockSpec((tm, d), lambda i, j: (i, 0)),
        pl.BlockSpec((1, d), lambda i, j: (0, 0)),
        pl.BlockSpec((mp, d), lambda i, j: (0, 0)),
        pl.BlockSpec((mp, d), lambda i, j: (0, 1)),
        pl.BlockSpec((d, tn), lambda i, j: (0, j)),
    ]
    out_specs = [pl.BlockSpec((tm, tn), lambda i, j: (i, j))]
    out_shape = [jax.ShapeDtypeStruct((t, n), BF16)]
    args = [xs, g.reshape(1, d), mods, mods, w]
    if with_gates:
        ng = wgt.shape[0]
        in_specs += [pl.BlockSpec((ng, d), lambda i, j: (0, 0)), pl.BlockSpec((ng, 1), lambda i, j: (0, 0))]
        out_specs.append(pl.BlockSpec((ng, tm), lambda i, j: (0, i)))
        out_shape.append(jax.ShapeDtypeStruct((ng, t), F32))
        args += [wgt, bg.reshape(ng, 1)]
    res = pl.pallas_call(
        functools.partial(_inproj_kernel, with_gates=with_gates, sel=sel),
        name="in_projection",
        grid=(t // tm, n // tn),
        in_specs=in_specs,
        out_specs=out_specs,
        out_shape=out_shape,
        scratch_shapes=[pltpu.VMEM((tm, d), BF16)],
        compiler_params=_cparams(("arbitrary", "arbitrary")),
    )(*args)
    return res if with_gates else res[0]


def _chunk_order(nc, ncc, reverse):
    ctx = list(range(ncc))
    lat = list(range(ncc, nc))
    return (ctx[::-1] + lat[::-1]) if reverse else (ctx + lat)


def _mlstm_kernel(q_ref, k_ref, v_ref, o_ref, gates_ref, gh_ref, out_ref, vaug, yacc, *, nc, ncc):
    L = SCAN_CHUNK
    d = HEAD_DIM
    scale = d ** -0.5
    p_rows = nc * L
    vaug[:, 0:d] = v_ref[...]
    lane = lax.broadcasted_iota(jnp.int32, (p_rows, d), 1)
    vaug[:, d:2 * d] = jnp.where(lane == 0, 1.0, 0.0).astype(BF16)
    gates = gates_ref[...]
    ii = lax.broadcasted_iota(jnp.int32, (L, L), 0)
    jj = lax.broadcasted_iota(jnp.int32, (L, L), 1)
    eye = (ii == jj).astype(F32)
    lane_row = lax.broadcasted_iota(jnp.int32, (gates.shape[1], L), 1)

    def to_col(rows):
        return lax.dot_general(eye, rows, _NT, precision=HIGHEST, preferred_element_type=F32)

    def running_max(x, reverse):
        k = 1
        while k < L:
            if reverse:
                shifted = jnp.where(lane_row < L - k, pltpu.roll(x, L - k, 1), -jnp.inf)
            else:
                shifted = jnp.where(lane_row >= k, pltpu.roll(x, k, 1), -jnp.inf)
            x = jnp.maximum(x, shifted)
            k *= 2
        return x

    pre = []
    for dirn in (0, 1):
        li = gates[2 * dirn]
        lf = _log_sigmoid(gates[2 * dirn + 1])
        mask = (jj <= ii) if dirn == 0 else (jj >= ii)
        mask_f = mask.astype(F32)
        b_row = lax.dot_general(lf, mask_f, _NT, precision=HIGHEST, preferred_element_type=F32)
        b_col = lax.dot_general(mask_f, lf, _NT, precision=HIGHEST, preferred_element_type=F32)
        a_row = li - b_row
        amax_row = running_max(a_row, dirn == 1)
        last = slice(L - 1, L) if dirn == 0 else slice(0, 1)
        pre.append(dict(mask=mask, a_row=a_row, a_col=to_col(li) - b_col, b_col=b_col, amax_col=to_col(amax_row),
                        b_end=b_row[:, last], amax_end=amax_row[:, last]))
    state = [(jnp.zeros((d, 2 * d), F32), jnp.zeros((1, 1), F32)) for _ in (0, 1)]
    orders = [_chunk_order(nc, ncc, False), _chunk_order(nc, ncc, True)]
    written = set()
    for step in range(nc):
        for dirn in (0, 1):
            c = orders[dirn][step]
            p = pre[dirn]
            c_state, m = state[dirn]
            sl = slice(c * L, (c + 1) * L)
            qc = q_ref[sl, :]
            kc = k_ref[sl, :]
            va = vaug[sl, :]
            b_end = p["b_end"][c:c + 1, :]
            amax_end = p["amax_end"][c:c + 1, :]
            mx = jnp.maximum(m, p["amax_col"][:, c:c + 1])
            w = jnp.exp(jnp.where(p["mask"], p["a_row"][c:c + 1, :] - mx, -jnp.inf)) * scale
            w_inter = jnp.exp(m - mx)
            qk = lax.dot_general(qc, kc, _NT, preferred_element_type=F32)
            s = (qk * w).astype(BF16)
            r = jnp.dot(s, va, preferred_element_type=F32)
            if step > 0:
                r = r + w_inter * jnp.dot(qc, c_state.astype(BF16), preferred_element_type=F32)
            h = r[:, 0:d] / jnp.maximum(jnp.abs(r[:, d:d + 1]), jnp.exp(-(p["b_col"][:, c:c + 1] + mx)))
            if c in written:
                yacc[sl, :] = yacc[sl, :] + h
            else:
                yacc[sl, :] = h
                written.add(c)
            if step == nc - 1:
                continue
            m_end = jnp.maximum(m, amax_end)
            w_end = jnp.exp(p["a_col"][:, c:c + 1] - m_end) * scale
            w_prev = jnp.exp(m - m_end)
            kw = (kc.astype(F32) * w_end).astype(BF16)
            c_state = w_prev * c_state + lax.dot_general(kw, va, _TN, preferred_element_type=F32)
            state[dirn] = (c_state, b_end + m_end)
    y = yacc[...]
    yn = y * lax.rsqrt(jnp.mean(y * y, axis=-1, keepdims=True) + EPS) * gh_ref[...]
    out_ref[...] = (_sigmoid(o_ref[...].astype(F32)) * yn).astype(out_ref.dtype)


def _mlstm(proj, gates, g_head, *, nb, heads, p_rows, lc, width):
    nc = p_rows // SCAN_CHUNK
    ncc = lc // SCAN_CHUNK
    ncp = gates.shape[3]
    d = HEAD_DIM
    t = proj.shape[0]

    def col(base):
        return pl.BlockSpec((p_rows, d), lambda b, h: (b, base * heads + h))

    return pl.pallas_call(
        functools.partial(_mlstm_kernel, nc=nc, ncc=ncc),
        name="mlstm",
        grid=(nb, heads),
        in_specs=[col(0), col(1), col(2), col(3),
                  pl.BlockSpec((4, None, None, ncp, SCAN_CHUNK), lambda b, h: (0, h, b, 0, 0)),
                  pl.BlockSpec((None, 1, d), lambda b, h: (h, 0, 0))],
        out_specs=pl.BlockSpec((p_rows, d), lambda b, h: (b, h)),
        out_shape=jax.ShapeDtypeStruct((t, width), BF16),
        scratch_shapes=[pltpu.VMEM((p_rows, 2 * d), BF16), pltpu.VMEM((p_rows, d), F32)],
        compiler_params=_cparams(("arbitrary", "arbitrary")),
    )(proj, proj, proj, proj, gates, g_head.reshape(heads, 1, d))


def _softmax_pv(parts):
    d = HEAD_DIM
    m = parts[0][0].max(axis=-1, keepdims=True)
    for s, _ in parts[1:]:
        m = jnp.maximum(m, s.max(axis=-1, keepdims=True))
    acc = 0.0
    for s, va in parts:
        acc = acc + jnp.dot(jnp.exp((s - m).astype(BF16)), va, preferred_element_type=F32)
    return acc[:, 0:d] / acc[:, d:d + 1]


def _fill_v_with_ones_column(vaug, v_ref):
    d = HEAD_DIM
    vaug[:, 0:d] = v_ref[...]
    lane = lax.broadcasted_iota(jnp.int32, (vaug.shape[0], d), 1)
    vaug[:, d:2 * d] = jnp.where(lane == 0, 1.0, 0.0).astype(BF16)


def _natten_kernel(q_ref, k_ref, v_in_ref, bias_ref, out_ref, v_ref, *, lc, rows):
    scale = HEAD_DIM ** -0.5
    _fill_v_with_ones_column(v_ref, v_in_ref)
    kc = k_ref[0:lc, :]
    vc = v_ref[0:lc, :]
    s_cc = lax.dot_general(q_ref[0:lc, :], kc, _NT, preferred_element_type=F32) * scale
    out_ref[0:lc, :] = _softmax_pv([(s_cc, vc)]).astype(out_ref.dtype)
    n_blocks = rows // NA_Q_ROWS
    for rb in range(n_blocks):
        r = rb * NA_Q_ROWS
        ks = min(max(r - NA_WIN_R // 2, 0), rows - NA_K_ROWS)
        pattern = 0 if rb == 0 else (2 if rb == n_blocks - 1 else 1)
        q_sl = slice(lc + r * GRID_W, lc + (r + NA_Q_ROWS) * GRID_W)
        k_sl = slice(lc + ks * GRID_W, lc + (ks + NA_K_ROWS) * GRID_W)
        qr = q_ref[q_sl, :]
        kb = k_ref[k_sl, :]
        vb = v_ref[k_sl, :]
        s_lat = lax.dot_general(qr, kb, _NT, preferred_element_type=F32) * scale + bias_ref[pattern]
        s_ctx = lax.dot_general(qr, kc, _NT, preferred_element_type=F32) * scale
        out_ref[q_sl, :] = _softmax_pv([(s_lat, vb), (s_ctx, vc)]).astype(out_ref.dtype)


def _natten_bias(rpb, rows):
    assert rows % NA_Q_ROWS == 0 and rows >= NA_K_ROWS
    col = np.arange(GRID_W)
    c0 = np.clip(col - NA_WIN_C // 2, 0, GRID_W - NA_WIN_C)
    col_ok = (col[None, :] >= c0[:, None]) & (col[None, :] < c0[:, None] + NA_WIN_C)
    dc_idx = np.clip(col[None, :] - col[:, None], -(NA_WIN_C - 1), NA_WIN_C - 1) + NA_WIN_C - 1
    bias_c = jnp.where(col_ok[None, None], rpb[:, :, dc_idx], NEG_BIG)
    masked = jnp.full(bias_c[:, 0].shape, NEG_BIG, F32)
    half = NA_WIN_R // 2
    patterns = [(0, lambda a: 0), (-half, lambda a: a), (-NA_WIN_R, lambda a: half)]
    tables = []
    for delta, band0 in patterns:
        q_rows = []
        for a in range(NA_Q_ROWS):
            blocks = []
            for j in range(NA_K_ROWS):
                in_band = band0(a) <= j < band0(a) + NA_WIN_R
                blocks.append(bias_c[:, j + delta - a + NA_WIN_R - 1] if in_band else masked)
            q_rows.append(jnp.concatenate(blocks, axis=-1))
        tables.append(jnp.concatenate(q_rows, axis=-2))
    return jnp.stack(tables, axis=1)


def _natten(proj, bias, *, nb, heads, p_rows, lc, width, base):
    d = HEAD_DIM
    t = proj.shape[0]
    rows = (p_rows - lc) // GRID_W

    def col(k):
        return pl.BlockSpec((p_rows, d), lambda b, h: (b, (base + k) * heads + h))

    return pl.pallas_call(
        functools.partial(_natten_kernel, lc=lc, rows=rows),
        name="natten",
        grid=(nb, heads),
        in_specs=[col(0), col(1), col(2),
                  pl.BlockSpec((None,) + bias.shape[1:], lambda b, h: (h, 0, 0, 0))],
        out_specs=pl.BlockSpec((p_rows, d), lambda b, h: (b, h)),
        out_shape=jax.ShapeDtypeStruct((t, width), BF16),
        scratch_shapes=[pltpu.VMEM((p_rows, 2 * d), BF16)],
        compiler_params=_cparams(("arbitrary", "arbitrary")),
    )(proj, proj, proj, bias)


def _head_norm(x, g):
    return x * lax.rsqrt(jnp.mean(x * x, axis=-1, keepdims=True) + EPS) * g


def _gqa_kernel(q_ref, k_ref, v_ref, gq_ref, gk_ref, cos_ref, sin_ref, out_ref, kn_scr, vaug, *, lc, tq):
    d = HEAD_DIM
    scale = d ** -0.5
    qt = pl.program_id(2)
    ncc = lc // tq

    @pl.when(qt == 0)
    def _():
        kn = _head_norm(k_ref[...].astype(F32), gk_ref[...])
        kn_scr[0:lc, :] = kn[0:lc].astype(BF16)
        kn_scr[lc:, :] = _rope(kn[lc:], cos_ref[...], sin_ref[...]).astype(BF16)
        _fill_v_with_ones_column(vaug, v_ref)

    def attend(rotate, n_keys):
        for g in range(Q_PER_KV):
            qn = _head_norm(q_ref[:, g * d:(g + 1) * d].astype(F32), gq_ref[...])
            if rotate:
                off = pl.multiple_of((qt - ncc) * tq, tq)
                qn = _rope(qn, cos_ref[pl.ds(off, tq), :], sin_ref[pl.ds(off, tq), :])
            qb = (qn * scale).astype(BF16)
            s = lax.dot_general(qb, kn_scr[0:n_keys, :], _NT, preferred_element_type=F32)
            p = jnp.exp((s - s.max(axis=-1, keepdims=True)).astype(BF16))
            acc = jnp.dot(p, vaug[0:n_keys, :], preferred_element_type=F32)
            out_ref[:, g * d:(g + 1) * d] = (acc[:, 0:d] / acc[:, d:d + 1]).astype(out_ref.dtype)

    @pl.when(qt < ncc)
    def _():
        attend(False, lc)

    @pl.when(qt >= ncc)
    def _():
        attend(True, kn_scr.shape[0])


def _gqa(proj, g_q, g_k, cosf, sinf, *, nb, heads, p_rows, lc, width):
    d = HEAD_DIM
    kvh = heads // Q_PER_KV
    t = proj.shape[0]
    tq = 256
    nq = p_rows // tq
    s_rows = cosf.shape[0]
    gw = Q_PER_KV * d
    return pl.pallas_call(
        functools.partial(_gqa_kernel, lc=lc, tq=tq),
        name="gqa",
        grid=(nb, kvh, nq),
        in_specs=[pl.BlockSpec((tq, gw), lambda b, kh, i: (b * nq + i, kh)),
                  pl.BlockSpec((p_rows, d), lambda b, kh, i: (b, heads + kh)),
                  pl.BlockSpec((p_rows, d), lambda b, kh, i: (b, heads + kvh + kh)),
                  pl.BlockSpec((1, d), lambda b, kh, i: (0, 0)),
                  pl.BlockSpec((1, d), lambda b, kh, i: (0, 0)),
                  pl.BlockSpec((s_rows, d), lambda b, kh, i: (0, 0)),
                  pl.BlockSpec((s_rows, d), lambda b, kh, i: (0, 0))],
        out_specs=pl.BlockSpec((tq, gw), lambda b, kh, i: (b * nq + i, kh)),
        out_shape=jax.ShapeDtypeStruct((t, width), BF16),
        scratch_shapes=[pltpu.VMEM((p_rows, d), BF16), pltpu.VMEM((p_rows, 2 * d), BF16)],
        compiler_params=_cparams(("arbitrary", "arbitrary", "arbitrary")),
    )(proj, proj, proj, g_q.reshape(1, d), g_k.reshape(1, d), cosf, sinf)


def _retention_kernel(q_ref, k_ref, v_ref, g_ref, lg_ref, gh_ref, cos_ref, sin_ref, out_ref,
                      qs, ks, yacc, *, nc, ncc, lc):
    L = SCAN_CHUNK
    d = HEAD_DIM
    scale = d ** -0.5
    cosf = cos_ref[...]
    sinf = sin_ref[...]
    qs[0:lc, :] = q_ref[0:lc, :]
    ks[0:lc, :] = k_ref[0:lc, :]
    qs[lc:, :] = _rope(q_ref[lc:, :].astype(F32), cosf, sinf).astype(BF16)
    ks[lc:, :] = _rope(k_ref[lc:, :].astype(F32), cosf, sinf).astype(BF16)
    ii = lax.broadcasted_iota(jnp.int32, (L, L), 0)
    jj = lax.broadcasted_iota(jnp.int32, (L, L), 1)
    pos = lax.broadcasted_iota(jnp.int32, (L, 1), 0).astype(F32)
    decay_sum = 0.0
    dq, dk, dchunk = [], [], []
    for dirn in (0, 1):
        lg = _log_sigmoid(lg_ref[dirn:dirn + 1, 0:1])
        rel = (ii - jj) if dirn == 0 else (jj - ii)
        decay_sum = decay_sum + jnp.where(rel >= 0, jnp.exp(jnp.maximum(rel, 0).astype(F32) * lg), 0.0)
        if dirn == 0:
            dq.append(jnp.exp((pos + 1.0) * lg))
            dk.append(jnp.exp((L - 1.0 - pos) * lg) * scale)
        else:
            dq.append(jnp.exp((L - pos) * lg))
            dk.append(jnp.exp(pos * lg) * scale)
        dchunk.append(jnp.exp(L * lg))
    decay_sum = decay_sum * scale
    for c in range(nc):
        sl = slice(c * L, (c + 1) * L)
        s = (lax.dot_general(qs[sl, :], ks[sl, :], _NT, preferred_element_type=F32) * decay_sum).astype(BF16)
        yacc[sl, :] = jnp.dot(s, v_ref[sl, :], preferred_element_type=F32)
    state = [jnp.zeros((d, d), F32), jnp.zeros((d, d), F32)]
    orders = [_chunk_order(nc, ncc, False), _chunk_order(nc, ncc, True)]
    for step in range(nc):
        for dirn in (0, 1):
            c = orders[dirn][step]
            sl = slice(c * L, (c + 1) * L)
            kc = ks[sl, :]
            vc = v_ref[sl, :]
            if step > 0:
                inter = jnp.dot(qs[sl, :], state[dirn].astype(BF16), preferred_element_type=F32)
                yacc[sl, :] = yacc[sl, :] + dq[dirn] * inter
            if step < nc - 1:
                kw = (kc.astype(F32) * dk[dirn]).astype(BF16)
                state[dirn] = dchunk[dirn] * state[dirn] + lax.dot_general(kw, vc, _TN, preferred_element_type=F32)
    y = yacc[...]
    yn = y * lax.rsqrt(jnp.mean(y * y, axis=-1, keepdims=True) + EPS) * gh_ref[...]
    gate = g_ref[...].astype(F32)
    out_ref[...] = (gate * _sigmoid(gate) * yn).astype(out_ref.dtype)


def _retention(proj, decay_logit, g_head, cosf, sinf, *, nb, heads, p_rows, lc, width, base):
    d = HEAD_DIM
    nc = p_rows // SCAN_CHUNK
    ncc = lc // SCAN_CHUNK
    t = proj.shape[0]
    s_rows = cosf.shape[0]
    lg = jnp.broadcast_to(decay_logit.T[:, :, None], (heads, 2, d)).astype(F32)
    lg = jnp.concatenate([lg, jnp.zeros((heads, SUBLANES - 2, d), F32)], axis=1)

    def col(k):
        return pl.BlockSpec((p_rows, d), lambda b, h: (b, base + k * heads + h))

    return pl.pallas_call(
        functools.partial(_retention_kernel, nc=nc, ncc=ncc, lc=lc),
        name="retention",
        grid=(nb, heads),
        in_specs=[col(0), col(1), col(2), col(3),
                  pl.BlockSpec((None, SUBLANES, d), lambda b, h: (h, 0, 0)),
                  pl.BlockSpec((None, 1, d), lambda b, h: (h, 0, 0)),
                  pl.BlockSpec((s_rows, d), lambda b, h: (0, 0)),
                  pl.BlockSpec((s_rows, d), lambda b, h: (0, 0))],
        out_specs=pl.BlockSpec((p_rows, d), lambda b, h: (b, h)),
        out_shape=jax.ShapeDtypeStruct((t, width), BF16),
        scratch_shapes=[pltpu.VMEM((p_rows, d), BF16), pltpu.VMEM((p_rows, d), BF16),
                        pltpu.VMEM((p_rows, d), F32)],
        compiler_params=_cparams(("arbitrary", "arbitrary")),
    )(proj, proj, proj, proj, lg, g_head.reshape(heads, 1, d), cosf, sinf)


def _outproj_kernel(a_ref, b_ref, wa_ref, wb_ref, x_ref, gt_ref, o_ref, *, sel):
    i = pl.program_id(0)
    y = (jnp.dot(a_ref[...], wa_ref[...], preferred_element_type=F32)
         + jnp.dot(b_ref[...], wb_ref[...], preferred_element_type=F32))
    o_ref[...] = x_ref[...] + _select_rows(gt_ref, i, **sel) * y


def _out_projection(mix_a, mix_b, w_a, w_b, xs, mods, *, sel, gate_chunk):
    t, d = xs.shape
    gw = mix_a.shape[1]
    tm = sel["tm"]
    tn = min(1024, d)
    mp = mods.shape[0]
    npd = d // tn
    return pl.pallas_call(
        functools.partial(_outproj_kernel, sel=sel),
        name="out_projection",
        grid=(t // tm, npd),
        in_specs=[pl.BlockSpec((tm, gw), lambda i, j: (i, 0)),
                  pl.BlockSpec((tm, gw), lambda i, j: (i, 0)),
                  pl.BlockSpec((gw, tn), lambda i, j: (0, j)),
                  pl.BlockSpec((gw, tn), lambda i, j: (0, j)),
                  pl.BlockSpec((tm, tn), lambda i, j: (i, j)),
                  pl.BlockSpec((mp, tn), lambda i, j: (0, gate_chunk * npd + j))],
        out_specs=pl.BlockSpec((tm, tn), lambda i, j: (i, j)),
        out_shape=jax.ShapeDtypeStruct((t, d), F32),
        compiler_params=_cparams(("arbitrary", "arbitrary")),
    )(mix_a, mix_b, w_a, w_b, xs, mods)


def _first_argmax(vals):
    best = vals[0]
    idx = jnp.zeros_like(best)
    for j in range(1, len(vals)):
        upd = vals[j] > best
        idx = jnp.where(upd, float(j), idx)
        best = jnp.where(upd, vals[j], best)
    return idx, best


def _pick(idx, vals):
    out = vals[-1]
    for j in range(len(vals) - 2, -1, -1):
        out = jnp.where(idx == float(j), vals[j], out)
    return out


def _ffn_pre_kernel(x_ref, g_ref, sh_ref, sc_ref, wrt_ref, br_ref, h_ref, r_ref, *, sel):
    i = pl.program_id(0)
    h = _norm_mod(x_ref[...], g_ref[...], sh_ref, sc_ref, i, **sel)
    h_ref[...] = _pack_bf16_pairs(h)
    logits = lax.dot_general(wrt_ref[...], h, _NT, precision=HIGHEST, preferred_element_type=F32)
    aff = _sigmoid(logits)
    sel_s = aff + br_ref[...]
    a = [aff[e:e + 1, :] for e in range(N_EXPERTS)]
    s = [sel_s[e:e + 1, :] for e in range(N_EXPERTS)]
    n = EXPERTS_PER_GROUP
    scores = []
    for grp in range(N_GROUPS):
        v = s[grp * n:(grp + 1) * n]
        best = v[0] + v[1]
        for p in range(n):
            for q in range(p + 1, n):
                if (p, q) != (0, 1):
                    best = jnp.maximum(best, v[p] + v[q])
        scores.append(best)
    gi, _ = _first_argmax(scores)
    cv = [_pick(gi, [s[grp * n + j] for grp in range(N_GROUPS)]) for j in range(n)]
    av = [_pick(gi, [a[grp * n + j] for grp in range(N_GROUPS)]) for j in range(n)]
    i1, _ = _first_argmax(cv)
    cv2 = [jnp.where(i1 == float(j), -jnp.inf, cv[j]) for j in range(n)]
    i2, _ = _first_argmax(cv2)
    w1 = _pick(i1, av)
    w2 = _pick(i2, av)
    tot = w1 + w2
    r_ref[0:1, :] = gi * float(n) + i1
    r_ref[1:2, :] = gi * float(n) + i2
    r_ref[2:3, :] = w1 / tot
    r_ref[3:4, :] = w2 / tot
    r_ref[4:8, :] = jnp.zeros((4, r_ref.shape[1]), F32)


def _ffn_pre(xs, g, mods, w_router_t, b_router, *, sel):
    t, d = xs.shape
    tm = sel["tm"]
    mp = mods.shape[0]
    ne = w_router_t.shape[0]
    return pl.pallas_call(
        functools.partial(_ffn_pre_kernel, sel=sel),
        name="ffn_pre_router",
        grid=(t // tm,),
        in_specs=[pl.BlockSpec((tm, d), lambda i: (i, 0)),
                  pl.BlockSpec((1, d), lambda i: (0, 0)),
                  pl.BlockSpec((mp, d), lambda i: (0, 3)),
                  pl.BlockSpec((mp, d), lambda i: (0, 4)),
                  pl.BlockSpec((ne, d), lambda i: (0, 0)),
                  pl.BlockSpec((ne, 1), lambda i: (0, 0))],
        out_specs=[pl.BlockSpec((tm, d // 2), lambda i: (i, 0)),
                   pl.BlockSpec((SUBLANES, tm), lambda i: (0, i))],
        out_shape=[jax.ShapeDtypeStruct((t, d // 2), jnp.uint32), jax.ShapeDtypeStruct((SUBLANES, t), F32)],
        compiler_params=_cparams(("arbitrary",)),
    )(xs, g.reshape(1, d), mods, mods, w_router_t, b_router.reshape(ne, 1))


def _dispatch_metadata(route, tm):
    t = route.shape[1]
    na = 2 * t
    assert na % tm == 0
    i32 = jnp.int32
    ne = N_EXPERTS
    n_pad = ne * tm
    e = route[0:2].astype(i32).reshape(na)
    expert_ids = jnp.arange(ne, dtype=i32)
    counts = jnp.sum((e[:, None] == expert_ids[None, :]).astype(i32), axis=0)
    pad_end = jnp.cumsum((-counts) % tm)
    pad_id = jnp.arange(n_pad, dtype=i32)
    pad_key = jnp.sum((pad_id[:, None] >= pad_end[None, :]).astype(i32), axis=1)
    a_id = jnp.arange(na, dtype=i32)
    keys = jnp.concatenate([e, pad_key])
    dst = jnp.concatenate([a_id, na + pad_id])
    src = jnp.concatenate([jnp.where(a_id >= t, a_id - t, a_id), jnp.zeros((n_pad,), i32)])
    keys_s, dst_s, src_s = lax.sort((keys, dst, src), num_keys=1, is_stable=True)
    te = keys_s[0::tm]
    last_e = jnp.max(jnp.where(counts > 0, expert_ids, 0))
    te = jnp.where(te < ne, te, last_e)
    n_slots = na + n_pad
    _, slot_of = lax.sort((dst_s, jnp.arange(n_slots, dtype=i32)), num_keys=1)
    zeros = jnp.zeros((tm,), i32)
    te_x = jnp.concatenate([te[:1], te, te[-1:]]).astype(i32)
    src_x = jnp.concatenate([zeros, src_s, zeros]).astype(i32)
    return te_x, src_x, slot_of[:na]


def _row_copy(src_hbm, row, dst, dst_row, sem):
    return pltpu.make_async_copy(src_hbm.at[pl.ds(row, 1), :], dst.at[pl.ds(dst_row, 1), :], sem)


def _for_rows(n, fn, unroll=8):
    def body(r, carry):
        fn(r)
        return carry
    lax.fori_loop(0, n, body, 0, unroll=unroll)


def _moe_kernel(te_ref, src_ref, h_hbm, wg_ref, wu_ref, wd_ref, y_ref, x0, x1, gsem, *, tm, n_tiles):
    del te_ref
    i = pl.program_id(0)
    xbufs = (x0, x1)

    def gather_copy(entry, r, buf, sem):
        return _row_copy(h_hbm, src_ref[entry * tm + r], buf, r, sem)

    @pl.when(i == 0)
    def _():
        _for_rows(tm, lambda r: gather_copy(1, r, x0, gsem.at[0]).start())

    for par in (0, 1):
        @pl.when(jnp.logical_and(jnp.logical_and(i >= 1, i <= n_tiles), (i - 1) % 2 == par))
        def _(cur_x=xbufs[par], nxt_x=xbufs[1 - par], par=par):
            for r in range(tm):
                gather_copy(i + 1, r, nxt_x, gsem.at[1 - par]).start()
            _for_rows(tm, lambda r: gather_copy(0, r, cur_x, gsem.at[par]).wait())
            xb = _unpack_bf16_pairs(cur_x[...])
            g = jnp.dot(xb, wg_ref[...], preferred_element_type=F32)
            u = jnp.dot(xb, wu_ref[...], preferred_element_type=F32)
            act = (g * _sigmoid(g) * u).astype(BF16)
            y_ref[...] = jnp.dot(act, wd_ref[...], preferred_element_type=F32)

    @pl.when(i == n_tiles + 1)
    def _():
        p_dummy = n_tiles % 2
        _for_rows(tm, lambda r: gather_copy(0, r, xbufs[p_dummy], gsem.at[p_dummy]).wait())


def _moe_experts(h2, te_x, src_x, wg, wu, wd):
    t, dh = h2.shape
    d = wg.shape[1]
    tm = MOE_TILE
    n_tiles = te_x.shape[0] - 2
    ff = wg.shape[2]
    grid_spec = pltpu.PrefetchScalarGridSpec(
        num_scalar_prefetch=2,
        grid=(n_tiles + 2,),
        in_specs=[pl.BlockSpec(memory_space=pl.ANY),
                  pl.BlockSpec((None, d, ff), lambda i, te, src: (te[i], 0, 0)),
                  pl.BlockSpec((None, d, ff), lambda i, te, src: (te[i], 0, 0)),
                  pl.BlockSpec((None, ff, d), lambda i, te, src: (te[i], 0, 0))],
        out_specs=pl.BlockSpec((tm, d), lambda i, te, src: (jnp.clip(i - 1, 0, n_tiles - 1), 0)),
        scratch_shapes=[pltpu.VMEM((tm, dh), jnp.uint32), pltpu.VMEM((tm, dh), jnp.uint32),
                        pltpu.SemaphoreType.DMA((2,))],
    )
    return pl.pallas_call(
        functools.partial(_moe_kernel, tm=tm, n_tiles=n_tiles),
        name="moe_experts",
        grid_spec=grid_spec,
        out_shape=jax.ShapeDtypeStruct((n_tiles * tm, d), F32),
        compiler_params=_cparams(("arbitrary",)),
    )(te_x, src_x, h2, wg, wu, wd)


def _combine_kernel(slot_ref, y_hbm, w0_ref, w1_ref, x_ref, gt_ref, o_ref, yb0, yb1, sem, *, tm, nt, t_total, sel):
    i = pl.program_id(0)
    bufs = (yb0, yb1)

    def copy(tile, r, k, buf, s):
        return _row_copy(y_hbm, slot_ref[k * t_total + tile * tm + r], buf.at[k], r, s)

    def wait_tile(buf, s):
        _for_rows(tm, lambda r: [copy(0, r, k, buf, s).wait() for k in range(2)], unroll=4)

    @pl.when(i == 0)
    def _():
        _for_rows(tm, lambda r: [copy(0, r, k, yb0, sem.at[0]).start() for k in range(2)], unroll=4)

    for par in (0, 1):
        @pl.when(jnp.logical_and(i < nt, i % 2 == par))
        def _(cur=bufs[par], nxt=bufs[1 - par], par=par):
            for r in range(tm):
                for k in range(2):
                    copy(i + 1, r, k, nxt, sem.at[1 - par]).start(priority=k)
            wait_tile(cur, sem.at[par])
            mix = w0_ref[...] * cur[0] + w1_ref[...] * cur[1]
            o_ref[...] = x_ref[...] + _select_rows(gt_ref, i, **sel) * mix

    @pl.when(i == nt)
    def _():
        wait_tile(bufs[nt % 2], sem.at[nt % 2])


def _moe_combine(ys, slots, route, xs, mods, *, sel):
    t, d = xs.shape
    tm = sel["tm"]
    nt = t // tm
    mp = mods.shape[0]
    w0 = route[2].reshape(t, 1)
    w1 = route[3].reshape(t, 1)
    slots = jnp.concatenate([slots, jnp.zeros((tm,), jnp.int32)])

    def tile(i, sl):
        return (jnp.minimum(i, nt - 1), 0)

    grid_spec = pltpu.PrefetchScalarGridSpec(
        num_scalar_prefetch=1,
        grid=(nt + 1,),
        in_specs=[pl.BlockSpec(memory_space=pl.ANY),
                  pl.BlockSpec((tm, 1), tile),
                  pl.BlockSpec((tm, 1), tile),
                  pl.BlockSpec((tm, d), tile),
                  pl.BlockSpec((mp, d), lambda i, sl: (0, 5))],
        out_specs=pl.BlockSpec((tm, d), tile),
        scratch_shapes=[pltpu.VMEM((2, tm, d), F32), pltpu.VMEM((2, tm, d), F32), pltpu.SemaphoreType.DMA((2,))],
    )
    return pl.pallas_call(
        functools.partial(_combine_kernel, tm=tm, nt=nt, t_total=t, sel=sel),
        name="moe_combine",
        grid_spec=grid_spec,
        out_shape=jax.ShapeDtypeStruct((t, d), F32),
        compiler_params=_cparams(("arbitrary",)),
    )(slots, ys, w0, w1, xs, mods)


def _final_norm_kernel(x_ref, g_ref, o_ref):
    x = x_ref[...]
    o_ref[...] = x * lax.rsqrt(jnp.mean(x * x, axis=-1, keepdims=True) + EPS) * g_ref[...]


def _final_norm(xs, g, *, nb, p_rows, lc):
    t, d = xs.shape
    tm = 256
    tpb = p_rows // tm
    ncc = lc // tm
    lat = tpb - ncc
    return pl.pallas_call(
        _final_norm_kernel,
        name="final_norm",
        grid=(nb, lat),
        in_specs=[pl.BlockSpec((tm, d), lambda b, j: (b * tpb + ncc + j, 0)),
                  pl.BlockSpec((1, d), lambda b, j: (0, 0))],
        out_specs=pl.BlockSpec((tm, d), lambda b, j: (b * lat + j, 0)),
        out_shape=jax.ShapeDtypeStruct((nb * lat * tm, d), F32),
        compiler_params=_cparams(("arbitrary", "arbitrary")),
    )(xs, g.reshape(1, d))


def _rope_tables(n_tokens):
    tpos = np.arange(n_tokens)
    row = (tpos // GRID_W).astype(np.float32)
    col = (tpos % GRID_W).astype(np.float32)
    n_axis = HEAD_DIM // 4
    inv = jnp.asarray(ROPE_BASE, F32) ** (-jnp.arange(n_axis, dtype=F32) / n_axis)
    ang = jnp.concatenate([jnp.asarray(row)[:, None] * inv, jnp.asarray(col)[:, None] * inv], axis=-1)
    cos, sin = jnp.cos(ang), jnp.sin(ang)
    return jnp.concatenate([cos, cos], axis=-1), jnp.concatenate([-sin, sin], axis=-1)


def kernel(x, c, ctx, c_ctx, w_mod, b_mod, g_norm_mix, g_norm_ffn, g_norm_out, w_in_ab, b_gate_ab, g_mlstm, rpb_na, w_out_ab, w_in_cd, g_qnorm, g_knorm, ret_decay_logit, g_ret, w_out_cd, w_router, b_router, w_exp_gate, w_exp_up, w_exp_down):
    nb, s_len, d = x.shape
    lc = ctx.shape[1]
    p_rows = lc + s_len
    heads = d // (2 * HEAD_DIM)
    kvh = heads // Q_PER_KV
    gw = heads * HEAD_DIM
    depth = w_mod.shape[0]
    assert lc % SCAN_CHUNK == 0 and s_len % SCAN_CHUNK == 0 and heads % Q_PER_KV == 0

    xs = jnp.concatenate([ctx, x], axis=1).reshape(nb * p_rows, d)
    mp = -(-(nb + 1) // SUBLANES) * SUBLANES
    cc = jnp.concatenate([c, c_ctx[None, :], jnp.zeros((mp - nb - 1, d), F32)], axis=0)
    mods = _modulation(cc, w_mod, b_mod)

    tm = _row_tile(p_rows)
    sel = dict(lc=lc, tm=tm, tpb=p_rows // tm, nb=nb)
    sel_c = dict(lc=lc, tm=COMBINE_TILE, tpb=p_rows // COMBINE_TILE, nb=nb)
    cosf, sinf = _rope_tables(s_len)
    nc = p_rows // SCAN_CHUNK
    ncp = -(-nc // SUBLANES) * SUBLANES
    seq = dict(nb=nb, heads=heads, p_rows=p_rows, lc=lc, width=gw)
    w_router_t = w_router.T.astype(F32)

    for layer in range(depth):
        ml = mods[layer]
        p = layer // 2
        if layer % 2 == 0:
            w = w_in_ab[p]
            ng = 4 * heads
            w_main = jnp.concatenate([w[:, :4 * gw], w[:, 4 * gw + ng:]], axis=1).astype(BF16)
            w_gate_t = w[:, 4 * gw:4 * gw + ng].T.astype(BF16)
            proj, gates_t = _in_projection(xs, g_norm_mix[layer], ml, w_main, w_gate_t, b_gate_ab[p], sel=sel)
            gates = gates_t.reshape(4, heads, nb, nc, SCAN_CHUNK)
            gates = jnp.pad(gates, ((0, 0), (0, 0), (0, 0), (0, ncp - nc), (0, 0)))
            mix_a = _mlstm(proj, gates, g_mlstm[p], **seq)
            mix_b = _natten(proj, _natten_bias(rpb_na[p], s_len // GRID_W), base=4, **seq)
            w_out = w_out_ab[p]
        else:
            proj = _in_projection(xs, g_norm_mix[layer], ml, w_in_cd[p].astype(BF16), None, None, sel=sel)
            mix_a = _gqa(proj, g_qnorm[p], g_knorm[p], cosf, sinf, **seq)
            mix_b = _retention(proj, ret_decay_logit[p], g_ret[p], cosf, sinf, base=heads + 2 * kvh, **seq)
            w_out = w_out_cd[p]
        xs = _out_projection(mix_a, mix_b, w_out[:gw].astype(BF16), w_out[gw:].astype(BF16), xs, ml,
                             sel=sel, gate_chunk=2)
        h2, route = _ffn_pre(xs, g_norm_ffn[layer], ml, w_router_t, b_router, sel=sel)
        te_x, src_x, slots = _dispatch_metadata(route, MOE_TILE)
        ys = _moe_experts(h2, te_x, src_x, w_exp_gate[layer].astype(BF16),
                          w_exp_up[layer].astype(BF16), w_exp_down[layer].astype(BF16))
        xs = _moe_combine(ys, slots, route, xs, ml, sel=sel_c)
    return _final_norm(xs, g_norm_out, nb=nb, p_rows=p_rows, lc=lc).reshape(nb, s_len, d)
```
